```python
import jax, jax.numpy as jnp
from jax import lax
import numpy as np

D_MODEL = 1024
BATCH = 32
SEQ = 2048
DEPTH = 1
DEC_BATCH = 128
DEC_SEQ = 1
PAST_LEN = 8192
PAGE_SIZE = 128

SSM_EXPAND = 2
D_INNER = SSM_EXPAND * D_MODEL
SSM_HEAD_DIM = 64
SSM_HEADS = D_INNER // SSM_HEAD_DIM
SSM_GROUPS = 4
SSM_HEADS_PER_GROUP = SSM_HEADS // SSM_GROUPS
D_STATE = 128
CONV_WIDTH = 4
CONV_DIM = D_INNER + 2 * SSM_GROUPS * D_STATE
SSD_CHUNK = 128
ATTN_HEAD_DIM = 64
HEADS_PER_DIL_GROUP = 4
DIL_GROUPS = ((128, 1), (512, 4), (2048, 16))
N_ATTN_HEADS = HEADS_PER_DIL_GROUP * len(DIL_GROUPS)
ATTN_WIDTH = N_ATTN_HEADS * ATTN_HEAD_DIM
ATTN_OUT_WIDTH = HEADS_PER_DIL_GROUP * ATTN_HEAD_DIM
PROJ_SIZES = (D_INNER, CONV_DIM, SSM_HEADS, ATTN_WIDTH, ATTN_WIDTH, ATTN_WIDTH, D_MODEL, D_MODEL)
D_IN_PROJ = D_INNER + CONV_DIM + SSM_HEADS + 3 * ATTN_WIDTH + 2 * D_MODEL
N_EXPERTS = 32
TOP_K = 4
D_FF = D_MODEL
SWIGLU_LIMIT = 7.0
SWIGLU_ALPHA = 1.702
MOE_BLOCK = 128
ALPHA = (2.0 * DEPTH) ** 0.25
BETA = (8.0 * DEPTH) ** -0.25
NORM_EPS = 1e-5

kernel_name = "hybrid_ssd_dilated_alibi_moe_decode_step"


def layer_norm(x, g, b):
    xf = x.astype(jnp.float32)
    mu = jnp.mean(xf, axis=-1, keepdims=True)
    var = jnp.mean(jnp.square(xf - mu), axis=-1, keepdims=True)
    return ((xf - mu) * lax.rsqrt(var + NORM_EPS) * g + b).astype(x.dtype)


def group_rms_norm(y, g):
    shp = y.shape
    yg = y.reshape(shp[:-1] + (SSM_GROUPS, D_INNER // SSM_GROUPS))
    yg = yg * lax.rsqrt(jnp.mean(jnp.square(yg), axis=-1, keepdims=True) + NORM_EPS)
    return yg.reshape(shp) * g


def alibi_slopes():
    return jnp.asarray(2.0 ** (-8.0 * np.arange(1, N_ATTN_HEADS + 1) / N_ATTN_HEADS), dtype=jnp.float32)


def causal_dwconv(x_ext, w, b):
    y = lax.conv_general_dilated(x_ext, w[:, None, :].astype(x_ext.dtype), window_strides=(1,), padding='VALID',
                                 dimension_numbers=('NWC', 'WIO', 'NWC'), feature_group_count=CONV_DIM)
    return y + b


def ssd_scan(xs, dt, A, Bm, Cm, h0):
    f32 = jnp.float32
    Bsz, T = xs.shape[0], xs.shape[1]
    lc = min(SSD_CHUNK, T)
    nc = -(-T // lc)
    pad = nc * lc - T
    padt = lambda t: jnp.pad(t, [(0, 0), (0, pad)] + [(0, 0)] * (t.ndim - 2))
    G, R, P, N = SSM_GROUPS, SSM_HEADS_PER_GROUP, SSM_HEAD_DIM, D_STATE
    xdt = padt(xs.astype(f32) * dt[..., None]).reshape(Bsz, nc, lc, G, R, P)
    dA = padt(dt * A).reshape(Bsz, nc, lc, G, R)
    Bc = padt(Bm.astype(f32)).reshape(Bsz, nc, lc, G, N)
    Cc = padt(Cm.astype(f32)).reshape(Bsz, nc, lc, G, N)
    a_cum = jnp.cumsum(dA, axis=2)
    causal = jnp.tril(jnp.ones((lc, lc), dtype=bool))[None, None, :, :, None, None]
    seg = a_cum[:, :, :, None] - a_cum[:, :, None, :]
    decay_ls = jnp.exp(jnp.where(causal, seg, -jnp.inf))
    cb = jnp.einsum('bclgn,bcsgn->bclsg', Cc, Bc)
    y_diag = jnp.einsum('bclsg,bclsgr,bcsgrp->bclgrp', cb, decay_ls, xdt)
    decay_end = jnp.exp(a_cum[:, :, -1:] - a_cum)
    chunk_states = jnp.einsum('bclgn,bclgr,bclgrp->bcgrpn', Bc, decay_end, xdt)
    chunk_decay = jnp.exp(a_cum[:, :, -1])

    def step(h, inp):
        dec, st = inp
        return dec[..., None, None] * h + st, h

    h_last, h_prev = lax.scan(step, h0.astype(f32).reshape(Bsz, G, R, P, N),
                              (jnp.moveaxis(chunk_decay, 1, 0), jnp.moveaxis(chunk_states, 1, 0)))
    h_prev = jnp.moveaxis(h_prev, 0, 1)
    y_off = jnp.einsum('bclgn,bcgrpn,bclgr->bclgrp', Cc, h_prev, jnp.exp(a_cum))
    y = (y_diag + y_off).reshape(Bsz, nc * lc, SSM_HEADS, P)[:, :T]
    return y, h_last.reshape(Bsz, SSM_HEADS, P, N)


def dilated_attention_prompt(q, k, v, window, dil, slopes):
    f32 = jnp.float32
    Bsz, T, Hg, dh = q.shape
    steps = window // dil
    L = T // dil
    nb = -(-L // steps)
    Lp = nb * steps

    def by_residue(t, front):
        t = t.reshape(Bsz, L, dil, Hg, dh)
        return jnp.pad(t, ((0, 0), (front, Lp - L), (0, 0), (0, 0), (0, 0)))

    def key_blocks(t):
        tb = by_residue(t, steps).reshape(Bsz, nb + 1, steps, dil, Hg, dh)
        return jnp.concatenate([tb[:, :-1], tb[:, 1:]], axis=2)

    scale = ATTN_HEAD_DIM ** -0.5
    qb = by_residue(q * scale, 0).reshape(Bsz, nb, steps, dil, Hg, dh)
    kb, vb = key_blocks(k), key_blocks(v)
    i = jnp.arange(steps)[:, None]
    j = jnp.arange(2 * steps)[None, :]
    d_sub = i + steps - j
    key_sub = jnp.arange(nb)[:, None, None] * steps + j[None] - steps
    valid = (d_sub >= 0) & (d_sub <= steps) & (key_sub >= 0)
    dist = (d_sub * dil).astype(f32)
    s = jnp.einsum('bnidhe,bnjdhe->bndhij', qb, kb, preferred_element_type=f32)
    s = s - slopes[:, None, None] * dist
    s = jnp.where(valid[None, :, None, None], s, -jnp.inf)
    lse = jax.nn.logsumexp(s, axis=-1)
    p = jnp.exp(s - lse[..., None])
    o = jnp.einsum('bndhij,bnjdhe->bnidhe', p.astype(vb.dtype), vb, preferred_element_type=f32)
    o = o.reshape(Bsz, Lp, dil, Hg, dh)[:, :L].reshape(Bsz, T, Hg, dh)
    lse = jnp.moveaxis(lse, -1, 2).reshape(Bsz, Lp, dil, Hg)[:, :L].reshape(Bsz, T, Hg)
    kv = jnp.stack([k, v], axis=2)[:, max(T - window, 0):]
    kv = jnp.pad(kv, ((0, 0), (window - kv.shape[1], 0), (0, 0), (0, 0), (0, 0)))
    return o, lse, kv


def dilated_attention_sample(q, k, v, kv_buf, window, dil, slopes):
    f32 = jnp.float32
    Bsz, T, Hg, dh = q.shape
    steps = window // dil
    kv_all = jnp.concatenate([kv_buf.astype(k.dtype), jnp.stack([k, v], axis=2)], axis=1)
    back = jnp.arange(steps + 1) * dil
    idx = window + jnp.arange(T)[:, None] - back[None]
    kv_sel = kv_all[:, idx]
    valid = (PAST_LEN + jnp.arange(T)[:, None] - back[None]) >= 0
    s = jnp.einsum('bthe,btihe->bthi', q * (ATTN_HEAD_DIM ** -0.5), kv_sel[:, :, :, 0], preferred_element_type=f32)
    s = s - slopes[:, None] * back.astype(f32)
    s = jnp.where(valid[None, :, None, :], s, -jnp.inf)
    lse = jax.nn.logsumexp(s, axis=-1)
    p = jnp.exp(s - lse[..., None])
    o = jnp.einsum('bthi,btihe->bthe', p.astype(kv_sel.dtype), kv_sel[:, :, :, 1], preferred_element_type=f32)
    return o, lse, kv_all[:, T:]


def token_mixer(u, conv_buf, h0, kv_bufs, is_prompt, w_in, conv_w, conv_b, dt_bias, a_log, d_skip,
                ssm_norm_g, w_out_ssm, w_out_attn, w_out):
    Bsz, T, _ = u.shape
    proj = u @ w_in
    cuts = [int(c) for c in np.cumsum(PROJ_SIZES)[:-1]]
    z, xbc, dt_raw, q, k, v, gate_a, gate_b = jnp.split(proj, cuts, axis=-1)
    xbc_ext = jnp.concatenate([conv_buf.astype(xbc.dtype), xbc], axis=1)
    new_conv = xbc_ext[:, -(CONV_WIDTH - 1):]
    xbc_c = jax.nn.silu(causal_dwconv(xbc_ext, conv_w, conv_b))
    xs, Bm, Cm = jnp.split(xbc_c, [D_INNER, D_INNER + SSM_GROUPS * D_STATE], axis=-1)
    xs = xs.reshape(Bsz, T, SSM_HEADS, SSM_HEAD_DIM)
    Bm = Bm.reshape(Bsz, T, SSM_GROUPS, D_STATE)
    Cm = Cm.reshape(Bsz, T, SSM_GROUPS, D_STATE)
    dt = jax.nn.softplus(dt_raw.astype(jnp.float32) + dt_bias)
    A = -jnp.exp(a_log.astype(jnp.float32))
    y, h_new = ssd_scan(xs, dt, A, Bm, Cm, h0)
    y = y + d_skip[:, None] * xs
    y = y.reshape(Bsz, T, D_INNER) * jax.nn.silu(z.astype(jnp.float32))
    branch_a = group_rms_norm(y, ssm_norm_g).astype(u.dtype) @ w_out_ssm
    q = q.reshape(Bsz, T, N_ATTN_HEADS, ATTN_HEAD_DIM)
    k = k.reshape(Bsz, T, N_ATTN_HEADS, ATTN_HEAD_DIM)
    v = v.reshape(Bsz, T, N_ATTN_HEADS, ATTN_HEAD_DIM)
    slopes = alibi_slopes()
    outs, lses, new_kv = [], [], []
    for g, (window, dil) in enumerate(DIL_GROUPS):
        hs = slice(g * HEADS_PER_DIL_GROUP, (g + 1) * HEADS_PER_DIL_GROUP)
        if is_prompt:
            o, lse, kvb = dilated_attention_prompt(q[:, :, hs], k[:, :, hs], v[:, :, hs], window, dil, slopes[hs])
        else:
            o, lse, kvb = dilated_attention_sample(q[:, :, hs], k[:, :, hs], v[:, :, hs], kv_bufs[g], window, dil, slopes[hs])
        outs.append(o)
        lses.append(lse)
        new_kv.append(kvb)
    w_groups = jax.nn.softmax(jnp.stack(lses, axis=0), axis=0)
    o = jnp.einsum('gbth,gbthe->bthe', w_groups, jnp.stack(outs, axis=0))
    branch_b = o.reshape(Bsz, T, ATTN_OUT_WIDTH).astype(u.dtype) @ w_out_attn
    merged = jax.nn.sigmoid(gate_a) * branch_a + jax.nn.sigmoid(gate_b) * branch_b
    return merged @ w_out, new_conv, h_new, new_kv


def moe_ffn(x, w_router, b_router, w_gate_up, b_gate_up, w_down, b_down):
    shp = x.shape
    xt = x.reshape(-1, D_MODEL)
    n_tok = xt.shape[0]
    logits = (xt @ w_router).astype(jnp.float32) + b_router
    top_val, top_idx = lax.top_k(logits, TOP_K)
    gates = jax.nn.softmax(top_val, axis=-1)
    n_asg = n_tok * TOP_K
    e_flat = top_idx.reshape(-1)
    tok_flat = jnp.arange(n_asg, dtype=jnp.int32) // TOP_K
    order = jnp.argsort(e_flat)
    e_s, tok_s, g_s = e_flat[order], tok_flat[order], gates.reshape(-1)[order]
    counts = jnp.bincount(e_flat, length=N_EXPERTS)
    start = jnp.cumsum(counts) - counts
    padded = (counts + MOE_BLOCK - 1) // MOE_BLOCK * MOE_BLOCK
    pend = jnp.cumsum(padded)
    pstart = pend - padded
    dest = pstart[e_s] + jnp.arange(n_asg) - start[e_s]
    n_blk = -(-n_asg // MOE_BLOCK) + N_EXPERTS
    n_slot = n_blk * MOE_BLOCK
    buf_tok = jnp.full((n_slot,), n_tok, dtype=jnp.int32).at[dest].set(tok_s)
    buf_g = jnp.zeros((n_slot,), jnp.float32).at[dest].set(g_s)
    blk_exp = jnp.minimum(jnp.searchsorted(pend, jnp.arange(n_blk) * MOE_BLOCK, side='right'), N_EXPERTS - 1)
    xpad = jnp.concatenate([xt, jnp.zeros((1, D_MODEL), xt.dtype)], axis=0)

    def run_block(args):
        tok_b, g_b, e = args
        h = xpad[tok_b] @ w_gate_up[e] + b_gate_up[e]
        gate, up = h[:, :D_FF], h[:, D_FF:]
        gate = jnp.minimum(gate, SWIGLU_LIMIT)
        up = jnp.clip(up, -SWIGLU_LIMIT, SWIGLU_LIMIT)
        hmid = (up + 1.0) * (gate * jax.nn.sigmoid(SWIGLU_ALPHA * gate))
        out = hmid @ w_down[e] + b_down[e]
        return out * g_b[:, None].astype(out.dtype)

    outs = lax.map(run_block, (buf_tok.reshape(n_blk, MOE_BLOCK), buf_g.reshape(n_blk, MOE_BLOCK), blk_exp))
    y = jax.ops.segment_sum(outs.reshape(n_slot, D_MODEL), buf_tok, num_segments=n_tok + 1)[:n_tok]
    return y.reshape(shp).astype(x.dtype)


def decoder_layer(x, conv_buf, h0, kv_bufs, is_prompt, w_in, conv_w, conv_b, dt_bias, a_log, d_skip, ssm_norm_g,
                  w_out_ssm, w_out_attn, w_out, ln1_g, ln1_b, w_router, b_router, w_gate_up, b_gate_up, w_down,
                  b_down, ln2_g, ln2_b):
    mix, new_conv, h_new, new_kv = token_mixer(x, conv_buf, h0, kv_bufs, is_prompt, w_in, conv_w, conv_b, dt_bias,
                                               a_log, d_skip, ssm_norm_g, w_out_ssm, w_out_attn, w_out)
    x1 = layer_norm(ALPHA * x + mix.astype(x.dtype), ln1_g, ln1_b)
    x2 = layer_norm(ALPHA * x1 + moe_ffn(x1, w_router, b_router, w_gate_up, b_gate_up, w_down, b_down), ln2_g, ln2_b)
    return x2, h_new, new_conv, new_kv


def setup_inputs(seed: int = 0) -> dict:
    key = jax.random.key(seed)
    ks = jax.random.split(key, 32)
    f32 = jnp.float32
    nrm = lambda k, shape, s: jax.random.normal(k, shape, f32) * s
    L = DEPTH
    Hg, dh = HEADS_PER_DIL_GROUP, ATTN_HEAD_DIM
    v_lo = D_INNER + CONV_DIM + SSM_HEADS + 2 * ATTN_WIDTH
    col_scale = jnp.ones((D_IN_PROJ,), f32).at[v_lo:v_lo + ATTN_WIDTH].set(BETA)
    dt0 = jnp.exp(jax.random.uniform(ks[10], (L, SSM_HEADS), f32, np.log(1e-3), np.log(1e-1)))
    return {
        "x_prompt": nrm(ks[0], (BATCH, SEQ, D_MODEL), 1.0),
        "x_sample": nrm(ks[1], (DEC_BATCH, DEC_SEQ, D_MODEL), 1.0),
        "state_ssm": nrm(ks[2], (L, DEC_BATCH, SSM_HEADS, SSM_HEAD_DIM, D_STATE), 0.5),
        "state_conv": nrm(ks[3], (L, DEC_BATCH, CONV_WIDTH - 1, CONV_DIM), 1.0),
        "cache_kv_w128": nrm(ks[4], (L, DEC_BATCH, DIL_GROUPS[0][0], 2, Hg, dh), 1.0),
        "cache_kv_w512": nrm(ks[5], (L, DEC_BATCH, DIL_GROUPS[1][0], 2, Hg, dh), 1.0),
        "cache_kv_w2048": nrm(ks[6], (L, DEC_BATCH, DIL_GROUPS[2][0], 2, Hg, dh), 1.0),
        "w_in": nrm(ks[7], (L, D_MODEL, D_IN_PROJ), D_MODEL ** -0.5) * col_scale,
        "conv_w": nrm(ks[8], (L, CONV_WIDTH, CONV_DIM), CONV_WIDTH ** -0.5),
        "conv_b": nrm(ks[9], (L, CONV_DIM), 0.02),
        "dt_bias": dt0 + jnp.log(-jnp.expm1(-dt0)),
        "a_log": jnp.log(jax.random.uniform(ks[11], (L, SSM_HEADS), f32, 1.0, 16.0)),
        "d_skip": 1.0 + nrm(ks[12], (L, SSM_HEADS), 0.02),
        "ssm_norm_g": 1.0 + nrm(ks[13], (L, D_INNER), 0.02),
        "w_out_ssm": nrm(ks[14], (L, D_INNER, D_MODEL), D_INNER ** -0.5),
        "w_out_attn": nrm(ks[15], (L, ATTN_OUT_WIDTH, D_MODEL), ATTN_OUT_WIDTH ** -0.5),
        "w_out": nrm(ks[16], (L, D_MODEL, D_MODEL), BETA * D_MODEL ** -0.5),
        "ln1_g": 1.0 + nrm(ks[17], (L, D_MODEL), 0.02),
        "ln1_b": nrm(ks[18], (L, D_MODEL), 0.02),
        "w_router": nrm(ks[19], (L, D_MODEL, N_EXPERTS), D_MODEL ** -0.5),
        "b_router": nrm(ks[20], (L, N_EXPERTS), 0.01),
        "w_gate_up": nrm(ks[21], (L, N_EXPERTS, D_MODEL, 2 * D_FF), D_MODEL ** -0.5),
        "b_gate_up": nrm(ks[22], (L, N_EXPERTS, 2 * D_FF), 0.02),
        "w_down": nrm(ks[23], (L, N_EXPERTS, D_FF, D_MODEL), BETA * D_FF ** -0.5),
        "b_down": nrm(ks[24], (L, N_EXPERTS, D_MODEL), 0.02),
        "ln2_g": 1.0 + nrm(ks[25], (L, D_MODEL), 0.02),
        "ln2_b": nrm(ks[26], (L, D_MODEL), 0.02),
    }


def reference(x_prompt, x_sample, state_ssm, state_conv, cache_kv_w128, cache_kv_w512, cache_kv_w2048, w_in, conv_w,
              conv_b, dt_bias, a_log, d_skip, ssm_norm_g, w_out_ssm, w_out_attn, w_out, ln1_g, ln1_b, w_router,
              b_router, w_gate_up, b_gate_up, w_down, b_down, ln2_g, ln2_b):
    y_p, y_s = x_prompt, x_sample
    bp = x_prompt.shape[0]
    p_states, s_states = [], []
    for l in range(DEPTH):
        lw = (w_in[l], conv_w[l], conv_b[l], dt_bias[l], a_log[l], d_skip[l], ssm_norm_g[l], w_out_ssm[l],
              w_out_attn[l], w_out[l], ln1_g[l], ln1_b[l], w_router[l], b_router[l], w_gate_up[l], b_gate_up[l],
              w_down[l], b_down[l], ln2_g[l], ln2_b[l])
        conv0 = jnp.zeros((bp, CONV_WIDTH - 1, CONV_DIM), x_prompt.dtype)
        h0 = jnp.zeros((bp, SSM_HEADS, SSM_HEAD_DIM, D_STATE), jnp.float32)
        y_p, hp, cp, kvp = decoder_layer(y_p, conv0, h0, None, True, *lw)
        y_s, hs, cs, kvs = decoder_layer(y_s, state_conv[l], state_ssm[l],
                                         (cache_kv_w128[l], cache_kv_w512[l], cache_kv_w2048[l]), False, *lw)
        p_states.append((hp, cp, kvp[0], kvp[1], kvp[2]))
        s_states.append((hs, cs, kvs[0], kvs[1], kvs[2]))
    prompt_ssm = jnp.stack([st[0] for st in p_states])
    prompt_conv = jnp.stack([st[1] for st in p_states])
    prompt_kv_w128 = jnp.stack([st[2] for st in p_states])
    prompt_kv_w512 = jnp.stack([st[3] for st in p_states])
    prompt_kv_w2048 = jnp.stack([st[4] for st in p_states])
    sample_ssm = jnp.stack([st[0] for st in s_states])
    sample_conv = jnp.stack([st[1] for st in s_states])
    sample_kv_w128 = jnp.stack([st[2] for st in s_states])
    sample_kv_w512 = jnp.stack([st[3] for st in s_states])
    sample_kv_w2048 = jnp.stack([st[4] for st in s_states])
    return (y_p, y_s, prompt_ssm, prompt_conv, prompt_kv_w128, prompt_kv_w512, prompt_kv_w2048,
            sample_ssm, sample_conv, sample_kv_w128, sample_kv_w512, sample_kv_w2048)
```

```python
import functools

import numpy as np
import jax
import jax.numpy as jnp
from jax import lax
from jax.experimental import pallas as pl
from jax.experimental.pallas import tpu as pltpu

F32 = jnp.float32
BF16 = jnp.bfloat16

D_MODEL = 1024
D_INNER = 2048
SSM_HEAD_DIM = 64
SSM_HEADS = 32
SSM_GROUPS = 4
D_STATE = 128
CONV_WIDTH = 4
CONV_DIM = D_INNER + 2 * SSM_GROUPS * D_STATE
SSD_CHUNK = 128
ATTN_HEAD_DIM = 64
HEADS_PER_DIL_GROUP = 4
DIL_GROUPS = ((128, 1), (512, 4), (2048, 16))
N_ATTN_HEADS = HEADS_PER_DIL_GROUP * len(DIL_GROUPS)
ATTN_WIDTH = N_ATTN_HEADS * ATTN_HEAD_DIM
ATTN_OUT_WIDTH = HEADS_PER_DIL_GROUP * ATTN_HEAD_DIM
ATTN_STEPS = 128
N_EXPERTS = 32
TOP_K = 4
D_FF = D_MODEL
SWIGLU_LIMIT = 7.0
SWIGLU_ALPHA = 1.702
DEPTH = 1
ALPHA = (2.0 * DEPTH) ** 0.25
NORM_EPS = 1e-5

LANES = 128
SUBLANES = 8
VMEM_LIMIT = 56 * 1024 * 1024

COL_Z = 0
COL_XS = 2048
COL_GA = 4096
COL_GB = 5120
COL_B = 6144
COL_C = 6656
COL_Q = 7168
COL_K = 7936
COL_V = 8704
COL_DT = 9472
PROJ_W = 9728

MOE_ROWS = 256


def _cparams(n_grid):
    return pltpu.CompilerParams(dimension_semantics=("arbitrary",) * n_grid,
                                vmem_limit_bytes=VMEM_LIMIT)


def _sigmoid(x):
    return 1.0 / (1.0 + jnp.exp(-x))


def _silu(x):
    return x * _sigmoid(x)


def _softplus(x):
    return jnp.maximum(x, 0.0) + jnp.log1p(jnp.exp(-jnp.abs(x)))


def _layer_norm(v, g, b):
    mu = jnp.mean(v, axis=-1, keepdims=True)
    d = v - mu
    var = jnp.mean(d * d, axis=-1, keepdims=True)
    return d * lax.rsqrt(var + NORM_EPS) * g + b


def _dot_nt(a, b):
    return lax.dot_general(a, b, (((1,), (1,)), ((), ())), preferred_element_type=F32)


def _dot_tn(a, b):
    return lax.dot_general(a, b, (((0,), (0,)), ((), ())), preferred_element_type=F32)


def _dot_exact(a, b):
    return jnp.dot(a, b, preferred_element_type=F32, precision=lax.Precision.HIGHEST)


def _proj_body(x_ref, w_ref, o_ref, xb_ref):
    @pl.when(pl.program_id(1) == 0)
    def _():
        xb_ref[...] = x_ref[...].astype(BF16)

    o_ref[...] = jnp.dot(xb_ref[...], w_ref[...], preferred_element_type=F32)


def _in_proj(x2d, w_perm):
    m = x2d.shape[0]
    tm = min(1024, m)
    tn = 512
    assert m % tm == 0 and PROJ_W % tn == 0
    return pl.pallas_call(
        _proj_body,
        grid=(m // tm, PROJ_W // tn),
        in_specs=[pl.BlockSpec((tm, D_MODEL), lambda i, j: (i, 0)),
                  pl.BlockSpec((D_MODEL, tn), lambda i, j: (0, j))],
        out_specs=pl.BlockSpec((tm, tn), lambda i, j: (i, j)),
        out_shape=jax.ShapeDtypeStruct((m, PROJ_W), F32),
        scratch_shapes=[pltpu.VMEM((tm, D_MODEL), BF16)],
        compiler_params=_cparams(2),
    )(x2d, w_perm)


_PAIR = 2 * SSM_HEAD_DIM
_N_PAIRS = SSM_HEADS // 2
_PAIRS_PER_GROUP = _N_PAIRS // SSM_GROUPS
_EXT_PAD = SUBLANES


def _ssd_body(z_ref, xs_ref, b_ref, c_ref, dt_ref, cw_ref, cb_ref, dtb_ref, alog_ref, dskip_ref,
              ng_ref, yn_ref, hout_ref, convout_ref, ext_ref, xc_ref, s_ref, ybuf_ref):
    c = pl.program_id(1)
    nc = pl.num_programs(1)
    lc = SSD_CHUNK

    @pl.when(c == 0)
    def _():
        ext_ref[0:_EXT_PAD, :] = jnp.zeros((_EXT_PAD, CONV_DIM), F32)
        s_ref[...] = jnp.zeros(s_ref.shape, F32)

    ext_ref[_EXT_PAD:_EXT_PAD + lc, 0:D_INNER] = xs_ref[...]
    ext_ref[_EXT_PAD:_EXT_PAD + lc, D_INNER:D_INNER + 512] = b_ref[...]
    ext_ref[_EXT_PAD:_EXT_PAD + lc, D_INNER + 512:CONV_DIM] = c_ref[...]

    cstep = 256
    for c0 in range(0, CONV_DIM, cstep):
        cs = slice(c0, c0 + cstep)
        acc = cb_ref[:, cs] + cw_ref[CONV_WIDTH - 1:CONV_WIDTH, cs] * ext_ref[_EXT_PAD:_EXT_PAD + lc, cs]
        for s in range(1, CONV_WIDTH):
            acc = acc + (cw_ref[CONV_WIDTH - 1 - s:CONV_WIDTH - s, cs]
                         * ext_ref[_EXT_PAD - s:_EXT_PAD - s + lc, cs])
        xc_ref[:, cs] = _silu(acc)

    @pl.when(c == nc - 1)
    def _():
        convout_ref[0] = ext_ref[_EXT_PAD + lc - (CONV_WIDTH - 1):_EXT_PAD + lc, :]

    ext_ref[0:_EXT_PAD, :] = ext_ref[lc:lc + _EXT_PAD, :]

    dtv = _softplus(dt_ref[...] + dtb_ref[...])
    a_neg = -jnp.exp(alog_ref[...])
    d_a = dtv * a_neg
    row = lax.broadcasted_iota(jnp.int32, (lc, lc), 0)
    col = lax.broadcasted_iota(jnp.int32, (lc, lc), 1)
    causal = row >= col
    tril = causal.astype(F32)
    a_cum = _dot_exact(tril, d_a)
    a_cum_t = a_cum.T
    a_last = a_cum[lc - 1:lc, :]
    chunk_decay = jnp.exp(a_last)
    decay_end = jnp.exp(a_last - a_cum)
    exp_a = jnp.exp(a_cum)
    dte = dtv * decay_end

    lane = lax.broadcasted_iota(jnp.int32, (lc, LANES), 1)
    first = lane < SSM_HEAD_DIM
    lane_row = lax.broadcasted_iota(jnp.int32, (1, LANES), 1)
    first_row = lane_row < SSM_HEAD_DIM

    def pick(arr, h0):
        return jnp.where(first, arr[:, h0:h0 + 1], arr[:, h0 + 1:h0 + 2])

    for g in range(SSM_GROUPS):
        bg = xc_ref[:, D_INNER + g * D_STATE:D_INNER + (g + 1) * D_STATE]
        cg = xc_ref[:, D_INNER + 512 + g * D_STATE:D_INNER + 512 + (g + 1) * D_STATE]
        bg_b = bg.astype(BF16)
        cb = _dot_nt(cg.astype(BF16), bg_b)
        for j in range(_PAIRS_PER_GROUP):
            pi = g * _PAIRS_PER_GROUP + j
            h0 = 2 * pi
            xs_pair = xc_ref[:, pi * _PAIR:(pi + 1) * _PAIR]
            xdt = xs_pair * pick(dtv, h0)
            xdte = xs_pair * pick(dte, h0)
            lhs = []
            for h in (h0, h0 + 1):
                seg = a_cum[:, h:h + 1] - a_cum_t[h:h + 1, :]
                lhs.append((cb * jnp.exp(jnp.where(causal, seg, -jnp.inf))).astype(BF16))
            for h in (h0, h0 + 1):
                lhs.append((cg * exp_a[:, h:h + 1]).astype(BF16))
            lhs = jnp.concatenate(lhs, axis=1)
            s_old = s_ref[pi]
            zero = jnp.zeros_like(xdt)
            rhs = jnp.concatenate([jnp.where(first, xdt, zero), jnp.where(first, zero, xdt),
                                   jnp.where(first, s_old, zero), jnp.where(first, zero, s_old)],
                                  axis=0).astype(BF16)
            ybuf_ref[:, pi * _PAIR:(pi + 1) * _PAIR] = jnp.dot(lhs, rhs, preferred_element_type=F32)
            cd = jnp.where(first_row, chunk_decay[:, h0:h0 + 1], chunk_decay[:, h0 + 1:h0 + 2])
            s_ref[pi] = cd * s_old + _dot_tn(bg_b, xdte.astype(BF16))

    gw = D_INNER // SSM_GROUPS
    for g in range(SSM_GROUPS):
        gs = slice(g * gw, (g + 1) * gw)
        y = ybuf_ref[:, gs] + dskip_ref[:, gs] * xc_ref[:, gs]
        y = y * _silu(z_ref[:, gs])
        ms = jnp.mean(y * y, axis=-1, keepdims=True)
        yn_ref[:, gs] = (y * lax.rsqrt(ms + NORM_EPS) * ng_ref[:, gs]).astype(BF16)

    @pl.when(c == nc - 1)
    def _():
        for pi in range(_N_PAIRS):
            t = s_ref[pi].T
            hout_ref[0, 2 * pi] = t[0:SSM_HEAD_DIM]
            hout_ref[0, 2 * pi + 1] = t[SSM_HEAD_DIM:_PAIR]


def _ssd_prompt(proj, bsz, t, conv_w, conv_b, dtb_pad, alog_pad, dskip_x, norm_g):
    lc = SSD_CHUNK
    assert t % lc == 0
    nc = t // lc
    row = lambda b, c: b * nc + c
    const = lambda b, c: (0, 0)
    return pl.pallas_call(
        _ssd_body,
        grid=(bsz, nc),
        in_specs=[
            pl.BlockSpec((lc, D_INNER), lambda b, c: (row(b, c), COL_Z // D_INNER)),
            pl.BlockSpec((lc, D_INNER), lambda b, c: (row(b, c), COL_XS // D_INNER)),
            pl.BlockSpec((lc, 512), lambda b, c: (row(b, c), COL_B // 512)),
            pl.BlockSpec((lc, 512), lambda b, c: (row(b, c), COL_C // 512)),
            pl.BlockSpec((lc, LANES), lambda b, c: (row(b, c), COL_DT // LANES)),
            pl.BlockSpec((CONV_WIDTH, CONV_DIM), const),
            pl.BlockSpec((1, CONV_DIM), const),
            pl.BlockSpec((1, LANES), const),
            pl.BlockSpec((1, LANES), const),
            pl.BlockSpec((1, D_INNER), const),
            pl.BlockSpec((1, D_INNER), const),
        ],
        out_specs=[
            pl.BlockSpec((lc, D_INNER), lambda b, c: (row(b, c), 0)),
            pl.BlockSpec((1, SSM_HEADS, SSM_HEAD_DIM, D_STATE), lambda b, c: (b, 0, 0, 0)),
            pl.BlockSpec((1, CONV_WIDTH - 1, CONV_DIM), lambda b, c: (b, 0, 0)),
        ],
        out_shape=[
            jax.ShapeDtypeStruct((bsz * t, D_INNER), BF16),
            jax.ShapeDtypeStruct((bsz, SSM_HEADS, SSM_HEAD_DIM, D_STATE), F32),
            jax.ShapeDtypeStruct((bsz, CONV_WIDTH - 1, CONV_DIM), F32),
        ],
        scratch_shapes=[
            pltpu.VMEM((lc + _EXT_PAD, CONV_DIM), F32),
            pltpu.VMEM((lc, CONV_DIM), F32),
            pltpu.VMEM((_N_PAIRS, D_STATE, _PAIR), F32),
            pltpu.VMEM((lc, D_INNER), F32),
        ],
        compiler_params=_cparams(2),
    )(proj, proj, proj, proj, proj, conv_w, conv_b, dtb_pad, alog_pad, dskip_x, norm_g)


def _slope(head):
    return float(np.float32(2.0 ** (-8.0 * (head + 1) / N_ATTN_HEADS)))


def _attn_body(q_ref, kp_ref, k_ref, vp_ref, v_ref, o_ref, lse_ref, *, group, dil):
    n = pl.program_id(2)
    st = ATTN_STEPS
    dh = ATTN_HEAD_DIM
    i = lax.broadcasted_iota(jnp.int32, (st, 2 * st), 0)
    j = lax.broadcasted_iota(jnp.int32, (st, 2 * st), 1)
    d_sub = i + st - j
    valid = (d_sub >= 0) & (d_sub <= st) & ((j >= st) | (n > 0))
    dist = (d_sub * dil).astype(F32)
    q = q_ref[...] * (ATTN_HEAD_DIM ** -0.5)
    for h in range(HEADS_PER_DIL_GROUP):
        hs = slice(h * dh, (h + 1) * dh)
        qh = q[:, hs].astype(BF16)
        kh = jnp.concatenate([kp_ref[:, hs], k_ref[:, hs]], axis=0).astype(BF16)
        vh = jnp.concatenate([vp_ref[:, hs], v_ref[:, hs]], axis=0).astype(BF16)
        s = _dot_nt(qh, kh) - _slope(group * HEADS_PER_DIL_GROUP + h) * dist
        s = jnp.where(valid, s, -jnp.inf)
        m = jnp.max(s, axis=-1, keepdims=True)
        p = jnp.exp(s - m)
        l = jnp.sum(p, axis=-1, keepdims=True)
        o = jnp.dot(p.astype(BF16), vh, preferred_element_type=F32) / l
        o_ref[:, hs] = o
        lse_ref[:, hs] = jnp.broadcast_to(m + jnp.log(l), (st, dh))


def _attn_prompt(proj, bsz, t, group):
    window, dil = DIL_GROUPS[group]
    st = ATTN_STEPS
    assert window // dil == st and t % (dil * st) == 0
    sub = t // dil
    nb = sub // st
    wq = ATTN_OUT_WIDTH
    pv = proj.reshape(bsz * sub, dil * PROJ_W)
    cpr = PROJ_W // wq
    cur = lambda b, r, n: b * nb + n
    prev = lambda b, r, n: b * nb + jnp.maximum(n - 1, 0)
    colq = lambda r: r * cpr + COL_Q // wq + group
    colk = lambda r: r * cpr + COL_K // wq + group
    colv = lambda r: r * cpr + COL_V // wq + group
    o, lse = pl.pallas_call(
        functools.partial(_attn_body, group=group, dil=dil),
        grid=(bsz, dil, nb),
        in_specs=[
            pl.BlockSpec((st, wq), lambda b, r, n: (cur(b, r, n), colq(r))),
            pl.BlockSpec((st, wq), lambda b, r, n: (prev(b, r, n), colk(r))),
            pl.BlockSpec((st, wq), lambda b, r, n: (cur(b, r, n), colk(r))),
            pl.BlockSpec((st, wq), lambda b, r, n: (prev(b, r, n), colv(r))),
            pl.BlockSpec((st, wq), lambda b, r, n: (cur(b, r, n), colv(r))),
        ],
        out_specs=[pl.BlockSpec((st, wq), lambda b, r, n: (cur(b, r, n), r))] * 2,
        out_shape=[jax.ShapeDtypeStruct((bsz * sub, dil * wq), F32)] * 2,
        compiler_params=_cparams(3),
    )(pv, pv, pv, pv, pv)
    return o.reshape(bsz * t, wq), lse.reshape(bsz * t, wq)


def _mix_body(yn_ref, o0_ref, l0_ref, o1_ref, l1_ref, o2_ref, l2_ref, ga_ref, gb_ref, x_ref,
              wssm_ref, wattn_ref, wout_ref, g1_ref, b1_ref, wr_ref, br_ref,
              x1_ref, idx_ref, gate_ref):
    branch_a = jnp.dot(yn_ref[...], wssm_ref[...], preferred_element_type=F32)
    l0, l1, l2 = l0_ref[...], l1_ref[...], l2_ref[...]
    m = jnp.maximum(jnp.maximum(l0, l1), l2)
    e0, e1, e2 = jnp.exp(l0 - m), jnp.exp(l1 - m), jnp.exp(l2 - m)
    o = (e0 * o0_ref[...] + e1 * o1_ref[...] + e2 * o2_ref[...]) / (e0 + e1 + e2)
    branch_b = jnp.dot(o.astype(BF16), wattn_ref[...], preferred_element_type=F32)
    merged = _sigmoid(ga_ref[...]) * branch_a + _sigmoid(gb_ref[...]) * branch_b
    mix = jnp.dot(merged.astype(BF16), wout_ref[...], preferred_element_type=F32)
    x1 = _layer_norm(ALPHA * x_ref[...] + mix, g1_ref[...], b1_ref[...])
    x1_ref[...] = x1

    logits = jnp.dot(x1.astype(BF16), wr_ref[...], preferred_element_type=F32) + br_ref[...]
    lane = lax.broadcasted_iota(jnp.int32, logits.shape, 1)
    logits = jnp.where(lane < N_EXPERTS, logits, -jnp.inf)
    vals, idxs = [], []
    for _ in range(TOP_K):
        mk = jnp.max(logits, axis=-1, keepdims=True)
        ik = jnp.min(jnp.where(logits == mk, lane, LANES), axis=-1, keepdims=True)
        vals.append(mk)
        idxs.append(ik)
        logits = jnp.where(lane == ik, -jnp.inf, logits)
    es = [jnp.exp(v - vals[0]) for v in vals]
    den = es[0] + es[1] + es[2] + es[3]
    idx_out = jnp.zeros(lane.shape, jnp.int32)
    gate_out = jnp.zeros(lane.shape, F32)
    for k in range(TOP_K):
        idx_out = jnp.where(lane == k, idxs[k], idx_out)
        gate_out = jnp.where(lane == k, es[k] / den, gate_out)
    idx_ref[...] = idx_out
    gate_ref[...] = gate_out


def _mix(yn, attn, proj, x2d, w_ssm, w_attn, w_out, g1, b1, w_r, b_r):
    m = x2d.shape[0]
    tm = min(256, m)
    assert m % tm == 0
    rowblk = lambda w: pl.BlockSpec((tm, w), lambda i: (i, 0))
    const = lambda a: pl.BlockSpec(a.shape, lambda i: (0,) * a.ndim)
    attn_specs = [rowblk(ATTN_OUT_WIDTH)] * 6
    return pl.pallas_call(
        _mix_body,
        grid=(m // tm,),
        in_specs=[rowblk(D_INNER)] + attn_specs + [
            pl.BlockSpec((tm, D_MODEL), lambda i: (i, COL_GA // D_MODEL)),
            pl.BlockSpec((tm, D_MODEL), lambda i: (i, COL_GB // D_MODEL)),
            rowblk(D_MODEL),
            const(w_ssm), const(w_attn), const(w_out), const(g1), const(b1), const(w_r), const(b_r)],
        out_specs=[rowblk(D_MODEL), rowblk(LANES), rowblk(LANES)],
        out_shape=[jax.ShapeDtypeStruct((m, D_MODEL), F32),
                   jax.ShapeDtypeStruct((m, LANES), jnp.int32),
                   jax.ShapeDtypeStruct((m, LANES), F32)],
        compiler_params=_cparams(1),
    )(yn, *attn, proj, proj, x2d, w_ssm, w_attn, w_out, g1, b1, w_r, b_r)


def _row_gather(idx_ref, n_rows, src_hbm, dst, sem):
    def body(r, carry):
        t = idx_ref[0, 0, r]
        pltpu.make_async_copy(src_hbm.at[pl.ds(t, 1)], dst.at[pl.ds(r, 1)], sem).start()
        return carry
    lax.fori_loop(0, n_rows, body, 0)


def _row_gather_wait(n_rows, src_hbm, dst, sem):
    pltpu.make_async_copy(src_hbm.at[pl.ds(0, n_rows)], dst, sem).wait()


def _expert_body(blk_exp_ref, tok_cur_ref, tok_nxt_ref, x_hbm, wgu_ref, bgu_ref, wd_ref, bd_ref,
                 out_ref, xbuf, sem, wgu_b, wd_b):
    i = pl.program_id(0)
    n = pl.num_programs(0)
    tb = MOE_ROWS
    slot = lax.rem(i, 2)

    @pl.when(i == 0)
    def _():
        _row_gather(tok_cur_ref, tb, x_hbm, xbuf.at[0], sem.at[0])

    @pl.when(i + 1 < n)
    def _():
        _row_gather(tok_nxt_ref, tb, x_hbm, xbuf.at[1 - slot], sem.at[1 - slot])

    e = blk_exp_ref[i]
    e_prev = blk_exp_ref[jnp.maximum(i - 1, 0)]

    @pl.when((i == 0) | (e != e_prev))
    def _():
        wgu_b[...] = wgu_ref[0].astype(BF16)
        wd_b[...] = wd_ref[0].astype(BF16)

    _row_gather_wait(tb, x_hbm, xbuf.at[slot], sem.at[slot])
    x = xbuf[slot].astype(BF16)
    h = jnp.dot(x, wgu_b[...], preferred_element_type=F32) + bgu_ref[0]
    gate = jnp.minimum(h[:, :D_FF], SWIGLU_LIMIT)
    up = jnp.clip(h[:, D_FF:], -SWIGLU_LIMIT, SWIGLU_LIMIT)
    hmid = (up + 1.0) * (gate * _sigmoid(SWIGLU_ALPHA * gate))
    out_ref[...] = jnp.dot(hmid.astype(BF16), wd_b[...], preferred_element_type=F32) + bd_ref[0]


def _experts(x1, blk_exp, slot_tok, w_gate_up, b_gate_up, w_down, b_down):
    tb = MOE_ROWS
    n_blk = blk_exp.shape[0]
    tok3 = slot_tok.reshape(n_blk, 1, tb)
    grid_spec = pltpu.PrefetchScalarGridSpec(
        num_scalar_prefetch=1,
        grid=(n_blk,),
        in_specs=[
            pl.BlockSpec((1, 1, tb), lambda i, be: (i, 0, 0), memory_space=pltpu.SMEM),
            pl.BlockSpec((1, 1, tb), lambda i, be: (jnp.minimum(i + 1, n_blk - 1), 0, 0),
                         memory_space=pltpu.SMEM),
            pl.BlockSpec(memory_space=pl.ANY),
            pl.BlockSpec((1, D_MODEL, 2 * D_FF), lambda i, be: (be[i], 0, 0)),
            pl.BlockSpec((1, 1, 2 * D_FF), lambda i, be: (be[i], 0, 0)),
            pl.BlockSpec((1, D_FF, D_MODEL), lambda i, be: (be[i], 0, 0)),
            pl.BlockSpec((1, 1, D_MODEL), lambda i, be: (be[i], 0, 0)),
        ],
        out_specs=pl.BlockSpec((tb, D_MODEL), lambda i, be: (i, 0)),
        scratch_shapes=[
            pltpu.VMEM((2, tb, D_MODEL), F32),
            pltpu.SemaphoreType.DMA((2,)),
            pltpu.VMEM((D_MODEL, 2 * D_FF), BF16),
            pltpu.VMEM((D_FF, D_MODEL), BF16),
        ],
    )
    return pl.pallas_call(
        _expert_body,
        grid_spec=grid_spec,
        out_shape=jax.ShapeDtypeStruct((n_blk * tb, D_MODEL), F32),
        compiler_params=_cparams(1),
    )(blk_exp, tok3, tok3, x1, w_gate_up, b_gate_up.reshape(N_EXPERTS, 1, 2 * D_FF),
      w_down, b_down.reshape(N_EXPERTS, 1, D_MODEL))


_COMBINE_TOKENS = 128


def _combine_body(dst_cur_ref, dst_nxt_ref, rows_hbm, gate_ref, x1_ref, g2_ref, b2_ref, y_ref,
                  buf, sem):
    i = pl.program_id(0)
    n = pl.num_programs(0)
    tt = _COMBINE_TOKENS
    nr = TOP_K * tt
    slot = lax.rem(i, 2)

    @pl.when(i == 0)
    def _():
        _row_gather(dst_cur_ref, nr, rows_hbm, buf.at[0], sem.at[0])

    @pl.when(i + 1 < n)
    def _():
        _row_gather(dst_nxt_ref, nr, rows_hbm, buf.at[1 - slot], sem.at[1 - slot])

    _row_gather_wait(nr, rows_hbm, buf.at[slot], sem.at[slot])
    gates = gate_ref[...]
    acc = ALPHA * x1_ref[...]
    for k in range(TOP_K):
        acc = acc + gates[:, k:k + 1] * buf[slot, k * tt:(k + 1) * tt, :]
    y_ref[...] = _layer_norm(acc, g2_ref[...], b2_ref[...])


def _combine(rows, dest, gates_dense, x1, g2, b2):
    m = x1.shape[0]
    tt = _COMBINE_TOKENS
    assert m % tt == 0
    nt = m // tt
    dst3 = dest.reshape(nt, tt, TOP_K).transpose(0, 2, 1).reshape(nt, 1, TOP_K * tt)
    const = lambda a: pl.BlockSpec(a.shape, lambda i: (0,) * a.ndim)
    return pl.pallas_call(
        _combine_body,
        grid=(nt,),
        in_specs=[
            pl.BlockSpec((1, 1, TOP_K * tt), lambda i: (i, 0, 0), memory_space=pltpu.SMEM),
            pl.BlockSpec((1, 1, TOP_K * tt), lambda i: (jnp.minimum(i + 1, nt - 1), 0, 0),
                         memory_space=pltpu.SMEM),
            pl.BlockSpec(memory_space=pl.ANY),
            pl.BlockSpec((tt, LANES), lambda i: (i, 0)),
            pl.BlockSpec((tt, D_MODEL), lambda i: (i, 0)),
            const(g2), const(b2),
        ],
        out_specs=pl.BlockSpec((tt, D_MODEL), lambda i: (i, 0)),
        out_shape=jax.ShapeDtypeStruct((m, D_MODEL), F32),
        scratch_shapes=[pltpu.VMEM((2, TOP_K * tt, D_MODEL), F32), pltpu.SemaphoreType.DMA((2,))],
        compiler_params=_cparams(1),
    )(dst3, dst3, rows, gates_dense, x1, g2, b2)


def _route(top_idx):
    tb = MOE_ROWS
    n_tok = top_idx.shape[0]
    n_asg = n_tok * TOP_K
    e_flat = top_idx.reshape(-1)
    order = jnp.argsort(e_flat).astype(jnp.int32)
    counts = jnp.zeros((N_EXPERTS,), jnp.int32).at[e_flat].add(1)
    start = jnp.cumsum(counts) - counts
    padded = (counts + tb - 1) // tb * tb
    pend = jnp.cumsum(padded)
    pstart = pend - padded
    n_blk = -(-n_asg // tb) + N_EXPERTS
    blk_exp = jnp.minimum(jnp.searchsorted(pend, jnp.arange(n_blk, dtype=jnp.int32) * tb, side='right'),
                          N_EXPERTS - 1).astype(jnp.int32)
    slot = jnp.arange(n_blk * tb, dtype=jnp.int32)
    slot_e = jnp.repeat(blk_exp, tb)
    within = slot - pstart[slot_e]
    src = jnp.clip(start[slot_e] + within, 0, n_asg - 1)
    slot_tok = jnp.where(within < counts[slot_e], order[src] // TOP_K, 0).astype(jnp.int32)
    e_sorted = e_flat[order]
    dest_sorted = pstart[e_sorted] + jnp.arange(n_asg, dtype=jnp.int32) - start[e_sorted]
    dest = jnp.zeros((n_asg,), jnp.int32).at[order].set(dest_sorted, unique_indices=True)
    return blk_exp, slot_tok, dest.reshape(n_tok, TOP_K).astype(jnp.int32)


def _moe_and_norm(x1, idx_dense, gates_dense, w_gate_up, b_gate_up, w_down, b_down, g2, b2):
    blk_exp, slot_tok, dest = _route(idx_dense[:, :TOP_K])
    rows = _experts(x1, blk_exp, slot_tok, w_gate_up, b_gate_up, w_down, b_down)
    return _combine(rows, dest, gates_dense, x1, g2, b2)


def _expand_heads(v):
    r = lax.broadcasted_iota(jnp.int32, (LANES, D_INNER), 0)
    c = lax.broadcasted_iota(jnp.int32, (LANES, D_INNER), 1)
    sel = (c // SSM_HEAD_DIM == r).astype(F32)
    return _dot_exact(v, sel)


def _step_pre_body(xs_ref, b_ref, c_ref, dt_ref, conv_ref, cw_ref, cb_ref, dtb_ref, alog_ref,
                   xc_ref, convout_ref, xdt_ref, dec_ref):
    w = CONV_WIDTH
    for (c0, c1, src) in ((0, D_INNER, xs_ref), (D_INNER, D_INNER + 512, b_ref),
                          (D_INNER + 512, CONV_DIM, c_ref)):
        cs = slice(c0, c1)
        new = src[...]
        acc = cb_ref[:, cs] + cw_ref[w - 1:w, cs] * new
        for s in range(w - 1):
            acc = acc + cw_ref[s:s + 1, cs] * conv_ref[s, :, cs]
        xc_ref[:, cs] = _silu(acc)
        for s in range(w - 2):
            convout_ref[s, :, cs] = conv_ref[s + 1, :, cs]
        convout_ref[w - 2, :, cs] = new
    dtv = _softplus(dt_ref[...] + dtb_ref[...])
    dec = jnp.exp(dtv * (-jnp.exp(alog_ref[...])))
    xdt_ref[...] = xc_ref[:, 0:D_INNER] * _expand_heads(dtv)
    dec_ref[...] = _expand_heads(dec)


def _step_pre(proj, conv_t, conv_w, conv_b, dtb_pad, alog_pad):
    m = proj.shape[0]
    const2 = lambda a: pl.BlockSpec(a.shape, lambda i: (0,) * a.ndim)
    return pl.pallas_call(
        _step_pre_body,
        grid=(1,),
        in_specs=[
            pl.BlockSpec((m, D_INNER), lambda i: (0, COL_XS // D_INNER)),
            pl.BlockSpec((m, 512), lambda i: (0, COL_B // 512)),
            pl.BlockSpec((m, 512), lambda i: (0, COL_C // 512)),
            pl.BlockSpec((m, LANES), lambda i: (0, COL_DT // LANES)),
            const2(conv_t), const2(conv_w), const2(conv_b), const2(dtb_pad), const2(alog_pad)],
        out_specs=[
            pl.BlockSpec((m, CONV_DIM), lambda i: (0, 0)),
            pl.BlockSpec((CONV_WIDTH - 1, m, CONV_DIM), lambda i: (0, 0, 0)),
            pl.BlockSpec((m, D_INNER), lambda i: (0, 0)),
            pl.BlockSpec((m, D_INNER), lambda i: (0, 0))],
        out_shape=[
            jax.ShapeDtypeStruct((m, CONV_DIM), F32),
            jax.ShapeDtypeStruct((CONV_WIDTH - 1, m, CONV_DIM), F32),
            jax.ShapeDtypeStruct((m, D_INNER), F32),
            jax.ShapeDtypeStruct((m, D_INNER), F32)],
        compiler_params=_cparams(1),
    )(proj, proj, proj, proj, conv_t, conv_w, conv_b, dtb_pad, alog_pad)


_STEP_BT = 8


def _step_state_body(h_ref, xdt_ref, dec_ref, b_ref, c_ref, hout_ref, y_ref):
    rows = 128
    per_group = D_INNER // SSM_GROUPS
    lane = lax.broadcasted_iota(jnp.int32, (rows, _STEP_BT), 1)
    for r0 in range(0, D_INNER, rows):
        g = r0 // per_group
        ycols = jnp.zeros((rows, _STEP_BT), F32)
        for jb in range(_STEP_BT):
            brow = b_ref[jb:jb + 1, g * D_STATE:(g + 1) * D_STATE]
            crow = c_ref[jb:jb + 1, g * D_STATE:(g + 1) * D_STATE]
            xcol = xdt_ref[0, r0:r0 + rows, jb:jb + 1]
            dcol = dec_ref[0, r0:r0 + rows, jb:jb + 1]
            hn = dcol * h_ref[jb, r0:r0 + rows, :] + xcol * brow
            hout_ref[jb, r0:r0 + rows, :] = hn
            ycols = jnp.where(lane == jb, jnp.sum(hn * crow, axis=-1, keepdims=True), ycols)
        y_ref[0, r0:r0 + rows, :] = ycols


def _step_state(h0, xdt, dec, xc):
    m = h0.shape[0]
    bt = _STEP_BT
    assert m % bt == 0
    nb = m // bt
    cols = lambda a: a.reshape(nb, bt, D_INNER).transpose(0, 2, 1)
    h_new, y_cols = pl.pallas_call(
        _step_state_body,
        grid=(nb,),
        in_specs=[
            pl.BlockSpec((bt, D_INNER, D_STATE), lambda i: (i, 0, 0)),
            pl.BlockSpec((1, D_INNER, bt), lambda i: (i, 0, 0)),
            pl.BlockSpec((1, D_INNER, bt), lambda i: (i, 0, 0)),
            pl.BlockSpec((bt, 512), lambda i: (i, D_INNER // 512)),
            pl.BlockSpec((bt, 512), lambda i: (i, D_INNER // 512 + 1))],
        out_specs=[
            pl.BlockSpec((bt, D_INNER, D_STATE), lambda i: (i, 0, 0)),
            pl.BlockSpec((1, D_INNER, bt), lambda i: (i, 0, 0))],
        out_shape=[
            jax.ShapeDtypeStruct((m, D_INNER, D_STATE), F32),
            jax.ShapeDtypeStruct((nb, D_INNER, bt), F32)],
        compiler_params=_cparams(1),
    )(h0, cols(xdt), cols(dec), xc, xc)
    return h_new, y_cols.transpose(0, 2, 1).reshape(m, D_INNER)


def _step_post_body(y_ref, xc_ref, z_ref, dskip_ref, ng_ref, yn_ref):
    gw = D_INNER // SSM_GROUPS
    for g in range(SSM_GROUPS):
        gs = slice(g * gw, (g + 1) * gw)
        y = y_ref[:, gs] + dskip_ref[:, gs] * xc_ref[:, gs]
        y = y * _silu(z_ref[:, gs])
        ms = jnp.mean(y * y, axis=-1, keepdims=True)
        yn_ref[:, gs] = (y * lax.rsqrt(ms + NORM_EPS) * ng_ref[:, gs]).astype(BF16)


def _step_post(y, xc, proj, dskip_x, norm_g):
    m = y.shape[0]
    return pl.pallas_call(
        _step_post_body,
        grid=(1,),
        in_specs=[
            pl.BlockSpec((m, D_INNER), lambda i: (0, 0)),
            pl.BlockSpec((m, D_INNER), lambda i: (0, 0)),
            pl.BlockSpec((m, D_INNER), lambda i: (0, COL_Z // D_INNER)),
            pl.BlockSpec((1, D_INNER), lambda i: (0, 0)),
            pl.BlockSpec((1, D_INNER), lambda i: (0, 0))],
        out_specs=pl.BlockSpec((m, D_INNER), lambda i: (0, 0)),
        out_shape=jax.ShapeDtypeStruct((m, D_INNER), BF16),
        compiler_params=_cparams(1),
    )(y, xc, proj, dskip_x, norm_g)


def _step_attn_body(q_ref, kn_ref, vn_ref, kv0_ref, kv1_ref, kv2_ref,
                    o0_ref, l0_ref, o1_ref, l1_ref, o2_ref, l2_ref):
    st = ATTN_STEPS
    dh = ATTN_HEAD_DIM
    wq = ATTN_OUT_WIDTH
    jrow = lax.broadcasted_iota(jnp.int32, (1, st), 1)
    q_all = q_ref[0] * (ATTN_HEAD_DIM ** -0.5)
    kn_all = kn_ref[0]
    vn_all = vn_ref[0]
    for g, (kv_ref, o_ref, l_ref) in enumerate(((kv0_ref, o0_ref, l0_ref), (kv1_ref, o1_ref, l1_ref),
                                                (kv2_ref, o2_ref, l2_ref))):
        dil = DIL_GROUPS[g][1]
        dist = ((st - jrow) * dil).astype(F32)
        for h in range(HEADS_PER_DIL_GROUP):
            cs = slice(g * wq + h * dh, g * wq + (h + 1) * dh)
            qh = q_all[:, cs]
            kh = kv_ref[0, :, h * dh:(h + 1) * dh].astype(BF16)
            vh = kv_ref[0, :, wq + h * dh:wq + (h + 1) * dh].astype(BF16)
            q8 = jnp.broadcast_to(qh, (SUBLANES, dh)).astype(BF16)
            s = _dot_nt(q8, kh)[0:1, :] - _slope(g * HEADS_PER_DIL_GROUP + h) * dist
            s_new = jnp.sum(qh * kn_all[:, cs], axis=-1, keepdims=True)
            m = jnp.maximum(jnp.max(s, axis=-1, keepdims=True), s_new)
            p = jnp.exp(s - m)
            p_new = jnp.exp(s_new - m)
            l = jnp.sum(p, axis=-1, keepdims=True) + p_new
            p8 = jnp.broadcast_to(p, (SUBLANES, st)).astype(BF16)
            o = jnp.dot(p8, vh, preferred_element_type=F32)[0:1, :] + p_new * vn_all[:, cs]
            o_ref[0, :, h * dh:(h + 1) * dh] = o / l
            l_ref[0, :, h * dh:(h + 1) * dh] = jnp.broadcast_to(m + jnp.log(l), (1, dh))


def _step_attn(q, k_new, v_new, kv_sel):
    m = q.shape[0]
    wq = ATTN_OUT_WIDTH
    r3 = lambda a: a.reshape(m, 1, a.shape[-1])
    vec = pl.BlockSpec((1, 1, ATTN_WIDTH), lambda i: (i, 0, 0))
    kvs = pl.BlockSpec((1, ATTN_STEPS, 2 * wq), lambda i: (i, 0, 0))
    outs = pl.pallas_call(
        _step_attn_body,
        grid=(m,),
        in_specs=[vec, vec, vec, kvs, kvs, kvs],
        out_specs=[pl.BlockSpec((1, 1, wq), lambda i: (i, 0, 0))] * 6,
        out_shape=[jax.ShapeDtypeStruct((m, 1, wq), F32)] * 6,
        compiler_params=_cparams(1),
    )(r3(q), r3(k_new), r3(v_new), *kv_sel)
    return [o.reshape(m, wq) for o in outs]


def _prep_weights(w_in, conv_b, dt_bias, a_log, d_skip, ssm_norm_g, w_out_ssm, w_out_attn, w_out,
                  ln1_g, ln1_b, w_router, b_router, ln2_g, ln2_b):
    cuts = np.cumsum((D_INNER, CONV_DIM, SSM_HEADS, ATTN_WIDTH, ATTN_WIDTH, ATTN_WIDTH, D_MODEL, D_MODEL))
    z, xbc, dt, q, k, v, ga, gb = jnp.split(w_in, [int(c) for c in cuts[:-1]], axis=1)
    xs, bm, cm = jnp.split(xbc, [D_INNER, D_INNER + 512], axis=1)
    dt_pad = jnp.zeros((D_MODEL, PROJ_W - COL_DT - SSM_HEADS), w_in.dtype)
    w_perm = jnp.concatenate([z, xs, ga, gb, bm, cm, q, k, v, dt, dt_pad], axis=1).astype(BF16)
    pad_heads = lambda a: jnp.pad(a.astype(F32), (0, LANES - SSM_HEADS)).reshape(1, LANES)
    row = lambda a: a.astype(F32).reshape(1, -1)
    return dict(
        w_perm=w_perm, conv_b=row(conv_b), dtb=pad_heads(dt_bias), alog=pad_heads(a_log),
        dskip=row(jnp.repeat(d_skip, SSM_HEAD_DIM)), norm_g=row(ssm_norm_g),
        w_ssm=w_out_ssm.astype(BF16), w_attn=w_out_attn.astype(BF16), w_out=w_out.astype(BF16),
        g1=row(ln1_g), b1=row(ln1_b),
        w_r=jnp.pad(w_router, ((0, 0), (0, LANES - N_EXPERTS))).astype(BF16),
        b_r=jnp.pad(b_router.astype(F32), (0, LANES - N_EXPERTS)).reshape(1, LANES),
        g2=row(ln2_g), b2=row(ln2_b))


def _kv_window(proj3, group, window):
    bsz, t, _ = proj3.shape
    wq = ATTN_OUT_WIDTH
    k = proj3[:, t - window:, COL_K + group * wq:COL_K + (group + 1) * wq]
    v = proj3[:, t - window:, COL_V + group * wq:COL_V + (group + 1) * wq]
    kv = jnp.stack([k, v], axis=2)
    return kv.reshape(bsz, window, 2, HEADS_PER_DIL_GROUP, ATTN_HEAD_DIM)


def _layer_prompt(x, p, conv_w, moe_w):
    bsz, t, _ = x.shape
    x2d = x.reshape(bsz * t, D_MODEL)
    proj = _in_proj(x2d, p['w_perm'])
    yn, h_new, conv_new = _ssd_prompt(proj, bsz, t, conv_w, p['conv_b'], p['dtb'], p['alog'],
                                      p['dskip'], p['norm_g'])
    attn = []
    for g in range(len(DIL_GROUPS)):
        attn.extend(_attn_prompt(proj, bsz, t, g))
    x1, idx_dense, gates_dense = _mix(yn, attn, proj, x2d, p['w_ssm'], p['w_attn'], p['w_out'],
                                      p['g1'], p['b1'], p['w_r'], p['b_r'])
    y = _moe_and_norm(x1, idx_dense, gates_dense, *moe_w, p['g2'], p['b2'])
    proj3 = proj.reshape(bsz, t, PROJ_W)
    kvs = [_kv_window(proj3, g, w) for g, (w, _) in enumerate(DIL_GROUPS)]
    return y.reshape(bsz, t, D_MODEL), h_new, conv_new, kvs


def _layer_step(x, conv_buf, h0, kv_bufs, p, conv_w, moe_w):
    m = x.shape[0]
    x2d = x.reshape(m, D_MODEL)
    proj = _in_proj(x2d, p['w_perm'])
    xc, conv_new_t, xdt, dec = _step_pre(proj, conv_buf.transpose(1, 0, 2), conv_w, p['conv_b'],
                                         p['dtb'], p['alog'])
    h_new, y = _step_state(h0.reshape(m, D_INNER, D_STATE), xdt, dec, xc)
    yn = _step_post(y, xc, proj, p['dskip'], p['norm_g'])
    wq = ATTN_OUT_WIDTH
    kv_sel, kv_new = [], []
    for g, (window, dil) in enumerate(DIL_GROUPS):
        buf = kv_bufs[g]
        kv_sel.append(buf[:, ::dil].reshape(m, ATTN_STEPS, 2 * wq))
        k = proj[:, COL_K + g * wq:COL_K + (g + 1) * wq]
        v = proj[:, COL_V + g * wq:COL_V + (g + 1) * wq]
        new = jnp.stack([k, v], axis=1).reshape(m, 1, 2, HEADS_PER_DIL_GROUP, ATTN_HEAD_DIM)
        kv_new.append(jnp.concatenate([buf[:, 1:], new], axis=1))
    attn = _step_attn(proj[:, COL_Q:COL_Q + ATTN_WIDTH], proj[:, COL_K:COL_K + ATTN_WIDTH],
                      proj[:, COL_V:COL_V + ATTN_WIDTH], kv_sel)
    x1, idx_dense, gates_dense = _mix(yn, attn, proj, x2d, p['w_ssm'], p['w_attn'], p['w_out'],
                                      p['g1'], p['b1'], p['w_r'], p['b_r'])
    y_out = _moe_and_norm(x1, idx_dense, gates_dense, *moe_w, p['g2'], p['b2'])
    h_new = h_new.reshape(m, SSM_HEADS, SSM_HEAD_DIM, D_STATE)
    return y_out.reshape(m, 1, D_MODEL), h_new, conv_new_t.transpose(1, 0, 2), kv_new


def kernel(x_prompt, x_sample, state_ssm, state_conv, cache_kv_w128, cache_kv_w512, cache_kv_w2048, w_in, conv_w, conv_b, dt_bias, a_log, d_skip, ssm_norm_g, w_out_ssm, w_out_attn, w_out, ln1_g, ln1_b, w_router, b_router, w_gate_up, b_gate_up, w_down, b_down, ln2_g, ln2_b):
    assert w_in.shape[0] == DEPTH == 1 and x_sample.shape[1] == 1
    l = 0
    p = _prep_weights(w_in[l], conv_b[l], dt_bias[l], a_log[l], d_skip[l], ssm_norm_g[l], w_out_ssm[l],
                      w_out_attn[l], w_out[l], ln1_g[l], ln1_b[l], w_router[l], b_router[l], ln2_g[l],
                      ln2_b[l])
    moe_w = (w_gate_up[l], b_gate_up[l], w_down[l], b_down[l])
    y_p, hp, cp, kvp = _layer_prompt(x_prompt, p, conv_w[l], moe_w)
    y_s, hs, cs, kvs = _layer_step(x_sample, state_conv[l], state_ssm[l],
                                   (cache_kv_w128[l], cache_kv_w512[l], cache_kv_w2048[l]), p,
                                   conv_w[l], moe_w)
    stack = lambda a: a[None]
    return (y_p, y_s, stack(hp), stack(cp), stack(kvp[0]), stack(kvp[1]), stack(kvp[2]),
            stack(hs), stack(cs), stack(kvs[0]), stack(kvs[1]), stack(kvs[2]))
```

```python
import functools

import numpy as np
import jax
import jax.numpy as jnp
from jax import lax
from jax.experimental import pallas as pl
from jax.experimental.pallas import tpu as pltpu

F32 = jnp.float32
BF16 = jnp.bfloat16

D_MODEL = 1024
D_INNER = 2048
SSM_HEAD_DIM = 64
SSM_HEADS = 32
SSM_GROUPS = 4
D_STATE = 128
CONV_WIDTH = 4
CONV_DIM = D_INNER + 2 * SSM_GROUPS * D_STATE
SSD_CHUNK = 128
ATTN_HEAD_DIM = 64
HEADS_PER_DIL_GROUP = 4
DIL_GROUPS = ((128, 1), (512, 4), (2048, 16))
N_ATTN_HEADS = HEADS_PER_DIL_GROUP * len(DIL_GROUPS)
ATTN_WIDTH = N_ATTN_HEADS * ATTN_HEAD_DIM
ATTN_OUT_WIDTH = HEADS_PER_DIL_GROUP * ATTN_HEAD_DIM
ATTN_STEPS = 128
N_EXPERTS = 32
TOP_K = 4
D_FF = D_MODEL
SWIGLU_LIMIT = 7.0
SWIGLU_ALPHA = 1.702
DEPTH = 1
ALPHA = (2.0 * DEPTH) ** 0.25
NORM_EPS = 1e-5

LANES = 128
SUBLANES = 8
VMEM_LIMIT = 56 * 1024 * 1024

COL_Z = 0
COL_XS = 2048
COL_GA = 4096
COL_GB = 5120
COL_B = 6144
COL_C = 6656
COL_Q = 7168
COL_K = 7936
COL_V = 8704
COL_DT = 9472
PROJ_W = 9728

MOE_ROWS = 256


def _cparams(n_grid):
    return pltpu.CompilerParams(dimension_semantics=("arbitrary",) * n_grid,
                                vmem_limit_bytes=VMEM_LIMIT)


def _sigmoid(x):
    return 1.0 / (1.0 + jnp.exp(-x))


def _silu(x):
    return x * _sigmoid(x)


def _softplus(x):
    return jnp.maximum(x, 0.0) + jnp.log1p(jnp.exp(-jnp.abs(x)))


def _layer_norm(v, g, b):
    mu = jnp.mean(v, axis=-1, keepdims=True)
    d = v - mu
    var = jnp.mean(d * d, axis=-1, keepdims=True)
    return d * lax.rsqrt(var + NORM_EPS) * g + b


def _dot_nt(a, b):
    return lax.dot_general(a, b, (((1,), (1,)), ((), ())), preferred_element_type=F32)


def _dot_tn(a, b):
    return lax.dot_general(a, b, (((0,), (0,)), ((), ())), preferred_element_type=F32)


def _dot_exact(a, b):
    return jnp.dot(a, b, preferred_element_type=F32, precision=lax.Precision.HIGHEST)


def _proj_body(x_ref, w_ref, o_ref, xb_ref):
    @pl.when(pl.program_id(1) == 0)
    def _():
        xb_ref[...] = x_ref[...].astype(BF16)

    o_ref[...] = jnp.dot(xb_ref[...], w_ref[...], preferred_element_type=F32)


def _in_proj(x2d, w_perm):
    m = x2d.shape[0]
    tm = min(2048, m)
    tn = 512
    assert m % tm == 0 and PROJ_W % tn == 0
    return pl.pallas_call(
        _proj_body,
        grid=(m // tm, PROJ_W // tn),
        in_specs=[pl.BlockSpec((tm, D_MODEL), lambda i, j: (i, 0)),
                  pl.BlockSpec((D_MODEL, tn), lambda i, j: (0, j))],
        out_specs=pl.BlockSpec((tm, tn), lambda i, j: (i, j)),
        out_shape=jax.ShapeDtypeStruct((m, PROJ_W), F32),
        scratch_shapes=[pltpu.VMEM((tm, D_MODEL), BF16)],
        compiler_params=_cparams(2),
    )(x2d, w_perm)


_PAIR = 2 * SSM_HEAD_DIM
_N_PAIRS = SSM_HEADS // 2
_PAIRS_PER_GROUP = _N_PAIRS // SSM_GROUPS
_EXT_PAD = SUBLANES


def _ssd_body(z_ref, xs_ref, b_ref, c_ref, dt_ref, cw_ref, cb_ref, dtb_ref, alog_ref, dskip_ref,
              ng_ref, yn_ref, hout_ref, convout_ref, ext_ref, xc_ref, s_ref, ybuf_ref):
    c = pl.program_id(1)
    nc = pl.num_programs(1)
    lc = SSD_CHUNK

    @pl.when(c == 0)
    def _():
        ext_ref[0:_EXT_PAD, :] = jnp.zeros((_EXT_PAD, CONV_DIM), F32)
        s_ref[...] = jnp.zeros(s_ref.shape, F32)

    ext_ref[_EXT_PAD:_EXT_PAD + lc, 0:D_INNER] = xs_ref[...]
    ext_ref[_EXT_PAD:_EXT_PAD + lc, D_INNER:D_INNER + 512] = b_ref[...]
    ext_ref[_EXT_PAD:_EXT_PAD + lc, D_INNER + 512:CONV_DIM] = c_ref[...]

    cstep = 256
    for c0 in range(0, CONV_DIM, cstep):
        cs = slice(c0, c0 + cstep)
        acc = cb_ref[:, cs] + cw_ref[CONV_WIDTH - 1:CONV_WIDTH, cs] * ext_ref[_EXT_PAD:_EXT_PAD + lc, cs]
        for s in range(1, CONV_WIDTH):
            acc = acc + (cw_ref[CONV_WIDTH - 1 - s:CONV_WIDTH - s, cs]
                         * ext_ref[_EXT_PAD - s:_EXT_PAD - s + lc, cs])
        xc_ref[:, cs] = _silu(acc)

    @pl.when(c == nc - 1)
    def _():
        convout_ref[0] = ext_ref[_EXT_PAD + lc - (CONV_WIDTH - 1):_EXT_PAD + lc, :]

    ext_ref[0:_EXT_PAD, :] = ext_ref[lc:lc + _EXT_PAD, :]

    dtv = _softplus(dt_ref[...] + dtb_ref[...])
    a_neg = -jnp.exp(alog_ref[...])
    d_a = dtv * a_neg
    row = lax.broadcasted_iota(jnp.int32, (lc, lc), 0)
    col = lax.broadcasted_iota(jnp.int32, (lc, lc), 1)
    causal = row >= col
    tril = causal.astype(F32)
    a_cum = _dot_exact(tril, d_a)
    a_cum_t = a_cum.T
    a_last = a_cum[lc - 1:lc, :]
    chunk_decay = jnp.exp(a_last)
    decay_end = jnp.exp(a_last - a_cum)
    exp_a = jnp.exp(a_cum)
    dte = dtv * decay_end

    lane = lax.broadcasted_iota(jnp.int32, (lc, LANES), 1)
    first = lane < SSM_HEAD_DIM
    lane_row = lax.broadcasted_iota(jnp.int32, (1, LANES), 1)
    first_row = lane_row < SSM_HEAD_DIM

    def pick(arr, h0):
        return jnp.where(first, arr[:, h0:h0 + 1], arr[:, h0 + 1:h0 + 2])

    for g in range(SSM_GROUPS):
        bg = xc_ref[:, D_INNER + g * D_STATE:D_INNER + (g + 1) * D_STATE]
        cg = xc_ref[:, D_INNER + 512 + g * D_STATE:D_INNER + 512 + (g + 1) * D_STATE]
        bg_b = bg.astype(BF16)
        cb = _dot_nt(cg.astype(BF16), bg_b)
        for j in range(_PAIRS_PER_GROUP):
            pi = g * _PAIRS_PER_GROUP + j
            h0 = 2 * pi
            xs_pair = xc_ref[:, pi * _PAIR:(pi + 1) * _PAIR]
            xdt = xs_pair * pick(dtv, h0)
            xdte = xs_pair * pick(dte, h0)
            lhs = []
            for h in (h0, h0 + 1):
                seg = a_cum[:, h:h + 1] - a_cum_t[h:h + 1, :]
                lhs.append((cb * jnp.exp(jnp.where(causal, seg, -jnp.inf))).astype(BF16))
            for h in (h0, h0 + 1):
                lhs.append((cg * exp_a[:, h:h + 1]).astype(BF16))
            lhs = jnp.concatenate(lhs, axis=1)
            s_old = s_ref[pi]
            zero = jnp.zeros_like(xdt)
            rhs = jnp.concatenate([jnp.where(first, xdt, zero), jnp.where(first, zero, xdt),
                                   jnp.where(first, s_old, zero), jnp.where(first, zero, s_old)],
                                  axis=0).astype(BF16)
            ybuf_ref[:, pi * _PAIR:(pi + 1) * _PAIR] = jnp.dot(lhs, rhs, preferred_element_type=F32)
            cd = jnp.where(first_row, chunk_decay[:, h0:h0 + 1], chunk_decay[:, h0 + 1:h0 + 2])
            s_ref[pi] = cd * s_old + _dot_tn(bg_b, xdte.astype(BF16))

    gw = D_INNER // SSM_GROUPS
    for g in range(SSM_GROUPS):
        gs = slice(g * gw, (g + 1) * gw)
        y = ybuf_ref[:, gs] + dskip_ref[:, gs] * xc_ref[:, gs]
        y = y * _silu(z_ref[:, gs])
        ms = jnp.mean(y * y, axis=-1, keepdims=True)
        yn_ref[:, gs] = (y * lax.rsqrt(ms + NORM_EPS) * ng_ref[:, gs]).astype(BF16)

    @pl.when(c == nc - 1)
    def _():
        for pi in range(_N_PAIRS):
            t = s_ref[pi].T
            hout_ref[0, 2 * pi] = t[0:SSM_HEAD_DIM]
            hout_ref[0, 2 * pi + 1] = t[SSM_HEAD_DIM:_PAIR]


def _ssd_prompt(proj, bsz, t, conv_w, conv_b, dtb_pad, alog_pad, dskip_x, norm_g):
    lc = SSD_CHUNK
    assert t % lc == 0
    nc = t // lc
    row = lambda b, c: b * nc + c
    const = lambda b, c: (0, 0)
    return pl.pallas_call(
        _ssd_body,
        grid=(bsz, nc),
        in_specs=[
            pl.BlockSpec((lc, D_INNER), lambda b, c: (row(b, c), COL_Z // D_INNER)),
            pl.BlockSpec((lc, D_INNER), lambda b, c: (row(b, c), COL_XS // D_INNER)),
            pl.BlockSpec((lc, 512), lambda b, c: (row(b, c), COL_B // 512)),
            pl.BlockSpec((lc, 512), lambda b, c: (row(b, c), COL_C // 512)),
            pl.BlockSpec((lc, LANES), lambda b, c: (row(b, c), COL_DT // LANES)),
            pl.BlockSpec((CONV_WIDTH, CONV_DIM), const),
            pl.BlockSpec((1, CONV_DIM), const),
            pl.BlockSpec((1, LANES), const),
            pl.BlockSpec((1, LANES), const),
            pl.BlockSpec((1, D_INNER), const),
            pl.BlockSpec((1, D_INNER), const),
        ],
        out_specs=[
            pl.BlockSpec((lc, D_INNER), lambda b, c: (row(b, c), 0)),
            pl.BlockSpec((1, SSM_HEADS, SSM_HEAD_DIM, D_STATE), lambda b, c: (b, 0, 0, 0)),
            pl.BlockSpec((1, CONV_WIDTH - 1, CONV_DIM), lambda b, c: (b, 0, 0)),
        ],
        out_shape=[
            jax.ShapeDtypeStruct((bsz * t, D_INNER), BF16),
            jax.ShapeDtypeStruct((bsz, SSM_HEADS, SSM_HEAD_DIM, D_STATE), F32),
            jax.ShapeDtypeStruct((bsz, CONV_WIDTH - 1, CONV_DIM), F32),
        ],
        scratch_shapes=[
            pltpu.VMEM((lc + _EXT_PAD, CONV_DIM), F32),
            pltpu.VMEM((lc, CONV_DIM), F32),
            pltpu.VMEM((_N_PAIRS, D_STATE, _PAIR), F32),
            pltpu.VMEM((lc, D_INNER), F32),
        ],
        compiler_params=_cparams(2),
    )(proj, proj, proj, proj, proj, conv_w, conv_b, dtb_pad, alog_pad, dskip_x, norm_g)


def _slope(head):
    return float(np.float32(2.0 ** (-8.0 * (head + 1) / N_ATTN_HEADS)))


_HEADS_PER_SLAB = LANES // ATTN_HEAD_DIM
_SLABS_PER_GROUP = HEADS_PER_DIL_GROUP // _HEADS_PER_SLAB


def _attn_body(q_ref, kp_ref, k_ref, vp_ref, v_ref, o_ref, lse_ref, kw_ref, vw_ref, *, group, dil):
    slab = pl.program_id(1)
    n = pl.program_id(2)
    st = ATTN_STEPS
    dh = ATTN_HEAD_DIM
    i = lax.broadcasted_iota(jnp.int32, (st, 2 * st), 0)
    j = lax.broadcasted_iota(jnp.int32, (st, 2 * st), 1)
    d_sub = i + st - j
    valid = (d_sub >= 0) & (d_sub <= st) & ((j >= st) | (n > 0))
    dist = (d_sub * dil).astype(F32)

    @pl.when(n == pl.num_programs(2) - 1)
    def _():
        kw_ref[...] = k_ref[...]
        vw_ref[...] = v_ref[...]

    for r in range(dil):
        rows = pl.ds(r, st, stride=dil) if dil > 1 else slice(None)
        q = q_ref[rows, :] * (ATTN_HEAD_DIM ** -0.5)
        k2 = jnp.concatenate([kp_ref[rows, :], k_ref[rows, :]], axis=0).astype(BF16)
        v2 = jnp.concatenate([vp_ref[rows, :], v_ref[rows, :]], axis=0).astype(BF16)
        o_parts, lse_parts = [], []
        for h in range(_HEADS_PER_SLAB):
            hs = slice(h * dh, (h + 1) * dh)
            base = group * HEADS_PER_DIL_GROUP + h
            slope = jnp.where(slab == 0, _slope(base), _slope(base + _HEADS_PER_SLAB))
            s = _dot_nt(q[:, hs].astype(BF16), k2[:, hs]) - slope * dist
            s = jnp.where(valid, s, -jnp.inf)
            m = jnp.max(s, axis=-1, keepdims=True)
            p = jnp.exp(s - m)
            l = jnp.sum(p, axis=-1, keepdims=True)
            o_parts.append(jnp.dot(p.astype(BF16), v2[:, hs], preferred_element_type=F32) / l)
            lse_parts.append(jnp.broadcast_to(m + jnp.log(l), (st, dh)))
        o_ref[rows, :] = jnp.concatenate(o_parts, axis=1)
        lse_ref[rows, :] = jnp.concatenate(lse_parts, axis=1)


def _attn_prompt(proj, bsz, t, group):
    window, dil = DIL_GROUPS[group]
    st = ATTN_STEPS
    rb = st * dil
    assert window == rb and t % rb == 0 and _SLABS_PER_GROUP == 2
    nb = t // rb
    wq = ATTN_OUT_WIDTH
    cur = lambda b, n: b * nb + n
    prev = lambda b, n: b * nb + jnp.maximum(n - 1, 0)
    col = lambda c0: c0 // LANES + group * _SLABS_PER_GROUP
    colq, colk, colv = col(COL_Q), col(COL_K), col(COL_V)
    return pl.pallas_call(
        functools.partial(_attn_body, group=group, dil=dil),
        grid=(bsz, _SLABS_PER_GROUP, nb),
        in_specs=[
            pl.BlockSpec((rb, LANES), lambda b, s, n: (cur(b, n), colq + s)),
            pl.BlockSpec((rb, LANES), lambda b, s, n: (prev(b, n), colk + s)),
            pl.BlockSpec((rb, LANES), lambda b, s, n: (cur(b, n), colk + s)),
            pl.BlockSpec((rb, LANES), lambda b, s, n: (prev(b, n), colv + s)),
            pl.BlockSpec((rb, LANES), lambda b, s, n: (cur(b, n), colv + s)),
        ],
        out_specs=[pl.BlockSpec((rb, LANES), lambda b, s, n: (cur(b, n), s)),
                   pl.BlockSpec((rb, LANES), lambda b, s, n: (cur(b, n), s)),
                   pl.BlockSpec((rb, LANES), lambda b, s, n: (b, s)),
                   pl.BlockSpec((rb, LANES), lambda b, s, n: (b, s))],
        out_shape=[jax.ShapeDtypeStruct((bsz * t, wq), F32),
                   jax.ShapeDtypeStruct((bsz * t, wq), F32),
                   jax.ShapeDtypeStruct((bsz * window, wq), F32),
                   jax.ShapeDtypeStruct((bsz * window, wq), F32)],
        compiler_params=_cparams(3),
    )(proj, proj, proj, proj, proj)


def _mix_body(yn_ref, o0_ref, l0_ref, o1_ref, l1_ref, o2_ref, l2_ref, ga_ref, gb_ref, x_ref,
              wssm_ref, wattn_ref, wout_ref, g1_ref, b1_ref, wr_ref, br_ref,
              x1_ref, x1p_ref, idx_ref, gate_ref, rank_ref, cnt_ref, carry_ref):
    @pl.when(pl.program_id(0) == 0)
    def _():
        carry_ref[...] = jnp.zeros(carry_ref.shape, F32)

    branch_a = jnp.dot(yn_ref[...], wssm_ref[...], preferred_element_type=F32)
    l0, l1, l2 = l0_ref[...], l1_ref[...], l2_ref[...]
    m = jnp.maximum(jnp.maximum(l0, l1), l2)
    e0, e1, e2 = jnp.exp(l0 - m), jnp.exp(l1 - m), jnp.exp(l2 - m)
    o = (e0 * o0_ref[...] + e1 * o1_ref[...] + e2 * o2_ref[...]) / (e0 + e1 + e2)
    branch_b = jnp.dot(o.astype(BF16), wattn_ref[...], preferred_element_type=F32)
    merged = _sigmoid(ga_ref[...]) * branch_a + _sigmoid(gb_ref[...]) * branch_b
    mix = jnp.dot(merged.astype(BF16), wout_ref[...], preferred_element_type=F32)
    x1 = _layer_norm(ALPHA * x_ref[...] + mix, g1_ref[...], b1_ref[...])
    x1_ref[...] = x1
    x1b = x1.astype(BF16)
    bits = pltpu.bitcast(x1b.astype(F32), jnp.uint32)
    half = D_MODEL // 2
    x1p_ref[...] = (bits[:, :half] >> 16) | (bits[:, half:] & jnp.uint32(0xFFFF0000))

    logits = jnp.dot(x1b, wr_ref[...], preferred_element_type=F32) + br_ref[...]
    lane = lax.broadcasted_iota(jnp.int32, logits.shape, 1)
    logits = jnp.where(lane < N_EXPERTS, logits, -jnp.inf)
    vals, idxs = [], []
    for _ in range(TOP_K):
        mk = jnp.max(logits, axis=-1, keepdims=True)
        ik = jnp.min(jnp.where(logits == mk, lane, LANES), axis=-1, keepdims=True)
        vals.append(mk)
        idxs.append(ik)
        logits = jnp.where(lane == ik, -jnp.inf, logits)
    es = [jnp.exp(v - vals[0]) for v in vals]
    den = es[0] + es[1] + es[2] + es[3]
    tm = lane.shape[0]
    onehot = jnp.zeros(lane.shape, F32)
    for k in range(TOP_K):
        onehot = onehot + (lane == idxs[k]).astype(F32)
    ri = lax.broadcasted_iota(jnp.int32, (tm, tm), 0)
    ci = lax.broadcasted_iota(jnp.int32, (tm, tm), 1)
    before = (ri > ci).astype(BF16)
    prefix = jnp.dot(before, onehot.astype(BF16), preferred_element_type=F32) + carry_ref[...]
    idx_out = jnp.zeros(lane.shape, jnp.int32)
    gate_out = jnp.zeros(lane.shape, F32)
    rank_out = jnp.zeros(lane.shape, jnp.int32)
    for k in range(TOP_K):
        rank_k = jnp.sum(jnp.where(lane == idxs[k], prefix, 0.0), axis=-1, keepdims=True)
        idx_out = jnp.where(lane == k, idxs[k], idx_out)
        gate_out = jnp.where(lane == k, es[k] / den, gate_out)
        rank_out = jnp.where(lane == k, rank_k.astype(jnp.int32), rank_out)
    idx_ref[...] = idx_out
    gate_ref[...] = gate_out
    rank_ref[...] = rank_out
    carry_ref[...] = carry_ref[...] + jnp.sum(onehot, axis=0, keepdims=True)
    cnt_ref[...] = carry_ref[...].astype(jnp.int32)


def _mix(yn, attn, proj, x2d, w_ssm, w_attn, w_out, g1, b1, w_r, b_r):
    m = x2d.shape[0]
    tm = min(256, m)
    assert m % tm == 0
    rowblk = lambda w: pl.BlockSpec((tm, w), lambda i: (i, 0))
    const = lambda a: pl.BlockSpec(a.shape, lambda i: (0,) * a.ndim)
    attn_specs = [rowblk(ATTN_OUT_WIDTH)] * 6
    return pl.pallas_call(
        _mix_body,
        grid=(m // tm,),
        in_specs=[rowblk(D_INNER)] + attn_specs + [
            pl.BlockSpec((tm, D_MODEL), lambda i: (i, COL_GA // D_MODEL)),
            pl.BlockSpec((tm, D_MODEL), lambda i: (i, COL_GB // D_MODEL)),
            rowblk(D_MODEL),
            const(w_ssm), const(w_attn), const(w_out), const(g1), const(b1), const(w_r), const(b_r)],
        out_specs=[rowblk(D_MODEL), rowblk(D_MODEL // 2), rowblk(LANES), rowblk(LANES), rowblk(LANES),
                   pl.BlockSpec((1, LANES), lambda i: (0, 0))],
        out_shape=[jax.ShapeDtypeStruct((m, D_MODEL), F32),
                   jax.ShapeDtypeStruct((m, D_MODEL // 2), jnp.uint32),
                   jax.ShapeDtypeStruct((m, LANES), jnp.int32),
                   jax.ShapeDtypeStruct((m, LANES), F32),
                   jax.ShapeDtypeStruct((m, LANES), jnp.int32),
                   jax.ShapeDtypeStruct((1, LANES), jnp.int32)],
        scratch_shapes=[pltpu.VMEM((1, LANES), F32)],
        compiler_params=_cparams(1),
    )(yn, *attn, proj, proj, x2d, w_ssm, w_attn, w_out, g1, b1, w_r, b_r)


def _row_gather(idx_ref, n_rows, src_hbm, dst, sem):
    def body(r, carry):
        t = idx_ref[0, 0, r]
        pltpu.make_async_copy(src_hbm.at[pl.ds(t, 1)], dst.at[pl.ds(r, 1)], sem).start()
        return carry
    lax.fori_loop(0, n_rows, body, 0, unroll=8)


def _row_gather_wait(n_rows, src_hbm, dst, sem):
    pltpu.make_async_copy(src_hbm.at[pl.ds(0, n_rows)], dst, sem).wait()


def _expert_body(blk_exp_ref, n_used_ref, tok_cur_ref, tok_nxt_ref, x_hbm, wgu_ref, bgu_ref, wd_ref,
                 bd_ref, out_ref, xbuf, sem, wgu_b, wd_b):
    i = pl.program_id(0)
    n_used = n_used_ref[0]
    tb = MOE_ROWS
    slot = lax.rem(i, 2)

    @pl.when(i == 0)
    def _():
        _row_gather(tok_cur_ref, tb, x_hbm, xbuf.at[0], sem.at[0])

    @pl.when(i + 1 < n_used)
    def _():
        _row_gather(tok_nxt_ref, tb, x_hbm, xbuf.at[1 - slot], sem.at[1 - slot])

    e = blk_exp_ref[i]
    e_prev = blk_exp_ref[jnp.maximum(i - 1, 0)]

    @pl.when((i == 0) | ((e != e_prev) & (i < n_used)))
    def _():
        wgu_b[...] = wgu_ref[0].astype(BF16)
        wd_b[...] = wd_ref[0].astype(BF16)

    @pl.when(i < n_used)
    def _():
        _row_gather_wait(tb, x_hbm, xbuf.at[slot], sem.at[slot])
        u = xbuf[slot]
        lo = pltpu.bitcast(u << 16, F32)
        hi = pltpu.bitcast(u & jnp.uint32(0xFFFF0000), F32)
        x = jnp.concatenate([lo, hi], axis=1).astype(BF16)
        h = jnp.dot(x, wgu_b[...], preferred_element_type=F32) + bgu_ref[0]
        gate = jnp.minimum(h[:, :D_FF], SWIGLU_LIMIT)
        up = jnp.clip(h[:, D_FF:], -SWIGLU_LIMIT, SWIGLU_LIMIT)
        hmid = (up + 1.0) * (gate * _sigmoid(SWIGLU_ALPHA * gate))
        out_ref[...] = jnp.dot(hmid.astype(BF16), wd_b[...], preferred_element_type=F32) + bd_ref[0]

    @pl.when(i >= n_used)
    def _():
        out_ref[...] = jnp.zeros(out_ref.shape, F32)


def _experts(x1p, blk_exp, n_used, slot_tok, w_gate_up, b_gate_up, w_down, b_down):
    tb = MOE_ROWS
    n_blk = blk_exp.shape[0]
    tok3 = slot_tok.reshape(n_blk, 1, tb)
    grid_spec = pltpu.PrefetchScalarGridSpec(
        num_scalar_prefetch=2,
        grid=(n_blk,),
        in_specs=[
            pl.BlockSpec((1, 1, tb), lambda i, be, nu: (i, 0, 0), memory_space=pltpu.SMEM),
            pl.BlockSpec((1, 1, tb), lambda i, be, nu: (jnp.minimum(i + 1, n_blk - 1), 0, 0),
                         memory_space=pltpu.SMEM),
            pl.BlockSpec(memory_space=pl.ANY),
            pl.BlockSpec((1, D_MODEL, 2 * D_FF), lambda i, be, nu: (be[i], 0, 0)),
            pl.BlockSpec((1, 1, 2 * D_FF), lambda i, be, nu: (be[i], 0, 0)),
            pl.BlockSpec((1, D_FF, D_MODEL), lambda i, be, nu: (be[i], 0, 0)),
            pl.BlockSpec((1, 1, D_MODEL), lambda i, be, nu: (be[i], 0, 0)),
        ],
        out_specs=pl.BlockSpec((tb, D_MODEL), lambda i, be, nu: (i, 0)),
        scratch_shapes=[
            pltpu.VMEM((2, tb, D_MODEL // 2), jnp.uint32),
            pltpu.SemaphoreType.DMA((2,)),
            pltpu.VMEM((D_MODEL, 2 * D_FF), BF16),
            pltpu.VMEM((D_FF, D_MODEL), BF16),
        ],
    )
    return pl.pallas_call(
        _expert_body,
        grid_spec=grid_spec,
        out_shape=jax.ShapeDtypeStruct((n_blk * tb, D_MODEL), F32),
        compiler_params=_cparams(1),
    )(blk_exp, n_used, tok3, tok3, x1p, w_gate_up, b_gate_up.reshape(N_EXPERTS, 1, 2 * D_FF),
      w_down, b_down.reshape(N_EXPERTS, 1, D_MODEL))


_COMBINE_TOKENS = 128


def _combine_body(dst_cur_ref, dst_nxt_ref, rows_hbm, gate_ref, x1_ref, g2_ref, b2_ref, y_ref,
                  buf, sem):
    i = pl.program_id(0)
    n = pl.num_programs(0)
    tt = _COMBINE_TOKENS
    nr = TOP_K * tt
    slot = lax.rem(i, 2)

    @pl.when(i == 0)
    def _():
        _row_gather(dst_cur_ref, nr, rows_hbm, buf.at[0], sem.at[0])

    @pl.when(i + 1 < n)
    def _():
        _row_gather(dst_nxt_ref, nr, rows_hbm, buf.at[1 - slot], sem.at[1 - slot])

    _row_gather_wait(nr, rows_hbm, buf.at[slot], sem.at[slot])
    gates = gate_ref[...]
    acc = ALPHA * x1_ref[...]
    for k in range(TOP_K):
        acc = acc + gates[:, k:k + 1] * buf[slot, k * tt:(k + 1) * tt, :]
    y_ref[...] = _layer_norm(acc, g2_ref[...], b2_ref[...])


def _combine(rows, dest, gates_dense, x1, g2, b2):
    m = x1.shape[0]
    tt = _COMBINE_TOKENS
    assert m % tt == 0
    nt = m // tt
    dst3 = dest.reshape(nt, tt, TOP_K).transpose(0, 2, 1).reshape(nt, 1, TOP_K * tt)
    const = lambda a: pl.BlockSpec(a.shape, lambda i: (0,) * a.ndim)
    return pl.pallas_call(
        _combine_body,
        grid=(nt,),
        in_specs=[
            pl.BlockSpec((1, 1, TOP_K * tt), lambda i: (i, 0, 0), memory_space=pltpu.SMEM),
            pl.BlockSpec((1, 1, TOP_K * tt), lambda i: (jnp.minimum(i + 1, nt - 1), 0, 0),
                         memory_space=pltpu.SMEM),
            pl.BlockSpec(memory_space=pl.ANY),
            pl.BlockSpec((tt, LANES), lambda i: (i, 0)),
            pl.BlockSpec((tt, D_MODEL), lambda i: (i, 0)),
            const(g2), const(b2),
        ],
        out_specs=pl.BlockSpec((tt, D_MODEL), lambda i: (i, 0)),
        out_shape=jax.ShapeDtypeStruct((m, D_MODEL), F32),
        scratch_shapes=[pltpu.VMEM((2, TOP_K * tt, D_MODEL), F32), pltpu.SemaphoreType.DMA((2,))],
        compiler_params=_cparams(1),
    )(dst3, dst3, rows, gates_dense, x1, g2, b2)


def _route(top_idx, rank, counts):
    tb = MOE_ROWS
    n_tok = top_idx.shape[0]
    n_asg = n_tok * TOP_K
    start = jnp.cumsum(counts) - counts
    padded = (counts + tb - 1) // tb * tb
    pend = jnp.cumsum(padded)
    pstart = pend - padded
    n_blk = -(-n_asg // tb) + N_EXPERTS
    blk_lo = jnp.arange(n_blk, dtype=jnp.int32) * tb
    blk_exp = jnp.minimum(jnp.sum(blk_lo[:, None] >= pend[None, :], axis=1), N_EXPERTS - 1).astype(jnp.int32)
    n_used = (pend[N_EXPERTS - 1] // tb).astype(jnp.int32).reshape(1)
    order = jnp.argsort(top_idx.reshape(-1)).astype(jnp.int32)
    slot = jnp.arange(n_blk * tb, dtype=jnp.int32)
    slot_e = jnp.repeat(blk_exp, tb)
    within = slot - pstart[slot_e]
    src = jnp.clip(start[slot_e] + within, 0, n_asg - 1)
    slot_tok = jnp.where(within < counts[slot_e], order[src] // TOP_K, 0).astype(jnp.int32)
    dest = (pstart[top_idx] + rank).astype(jnp.int32)
    return blk_exp, n_used, slot_tok, dest


def _moe_and_norm(x1, x1p, idx_dense, gates_dense, rank_dense, counts, w_gate_up, b_gate_up, w_down,
                  b_down, g2, b2):
    blk_exp, n_used, slot_tok, dest = _route(idx_dense[:, :TOP_K], rank_dense[:, :TOP_K],
                                             counts[0, :N_EXPERTS])
    rows = _experts(x1p, blk_exp, n_used, slot_tok, w_gate_up, b_gate_up, w_down, b_down)
    return _combine(rows, dest, gates_dense, x1, g2, b2)


def _expand_heads(v):
    r = lax.broadcasted_iota(jnp.int32, (LANES, D_INNER), 0)
    c = lax.broadcasted_iota(jnp.int32, (LANES, D_INNER), 1)
    sel = (c // SSM_HEAD_DIM == r).astype(F32)
    return _dot_exact(v, sel)


def _step_pre_body(xs_ref, b_ref, c_ref, dt_ref, conv_ref, cw_ref, cb_ref, dtb_ref, alog_ref,
                   xc_ref, convout_ref, xdt_ref, dec_ref):
    w = CONV_WIDTH
    for (c0, c1, src) in ((0, D_INNER, xs_ref), (D_INNER, D_INNER + 512, b_ref),
                          (D_INNER + 512, CONV_DIM, c_ref)):
        cs = slice(c0, c1)
        new = src[...]
        acc = cb_ref[:, cs] + cw_ref[w - 1:w, cs] * new
        for s in range(w - 1):
            acc = acc + cw_ref[s:s + 1, cs] * conv_ref[s, :, cs]
        xc_ref[:, cs] = _silu(acc)
        for s in range(w - 2):
            convout_ref[s, :, cs] = conv_ref[s + 1, :, cs]
        convout_ref[w - 2, :, cs] = new
    dtv = _softplus(dt_ref[...] + dtb_ref[...])
    dec = jnp.exp(dtv * (-jnp.exp(alog_ref[...])))
    xdt_ref[...] = xc_ref[:, 0:D_INNER] * _expand_heads(dtv)
    dec_ref[...] = _expand_heads(dec)


def _step_pre(proj, conv_t, conv_w, conv_b, dtb_pad, alog_pad):
    m = proj.shape[0]
    const2 = lambda a: pl.BlockSpec(a.shape, lambda i: (0,) * a.ndim)
    return pl.pallas_call(
        _step_pre_body,
        grid=(1,),
        in_specs=[
            pl.BlockSpec((m, D_INNER), lambda i: (0, COL_XS // D_INNER)),
            pl.BlockSpec((m, 512), lambda i: (0, COL_B // 512)),
            pl.BlockSpec((m, 512), lambda i: (0, COL_C // 512)),
            pl.BlockSpec((m, LANES), lambda i: (0, COL_DT // LANES)),
            const2(conv_t), const2(conv_w), const2(conv_b), const2(dtb_pad), const2(alog_pad)],
        out_specs=[
            pl.BlockSpec((m, CONV_DIM), lambda i: (0, 0)),
            pl.BlockSpec((CONV_WIDTH - 1, m, CONV_DIM), lambda i: (0, 0, 0)),
            pl.BlockSpec((m, D_INNER), lambda i: (0, 0)),
            pl.BlockSpec((m, D_INNER), lambda i: (0, 0))],
        out_shape=[
            jax.ShapeDtypeStruct((m, CONV_DIM), F32),
            jax.ShapeDtypeStruct((CONV_WIDTH - 1, m, CONV_DIM), F32),
            jax.ShapeDtypeStruct((m, D_INNER), F32),
            jax.ShapeDtypeStruct((m, D_INNER), F32)],
        compiler_params=_cparams(1),
    )(proj, proj, proj, proj, conv_t, conv_w, conv_b, dtb_pad, alog_pad)


_STEP_BT = 8


def _step_state_body(h_ref, xdt_ref, dec_ref, b_ref, c_ref, hout_ref, y_ref):
    rows = 128
    per_group = D_INNER // SSM_GROUPS
    lane = lax.broadcasted_iota(jnp.int32, (rows, _STEP_BT), 1)
    for r0 in range(0, D_INNER, rows):
        g = r0 // per_group
        ycols = jnp.zeros((rows, _STEP_BT), F32)
        for jb in range(_STEP_BT):
            brow = b_ref[jb:jb + 1, g * D_STATE:(g + 1) * D_STATE]
            crow = c_ref[jb:jb + 1, g * D_STATE:(g + 1) * D_STATE]
            xcol = xdt_ref[0, r0:r0 + rows, jb:jb + 1]
            dcol = dec_ref[0, r0:r0 + rows, jb:jb + 1]
            hn = dcol * h_ref[jb, r0:r0 + rows, :] + xcol * brow
            hout_ref[jb, r0:r0 + rows, :] = hn
            ycols = jnp.where(lane == jb, jnp.sum(hn * crow, axis=-1, keepdims=True), ycols)
        y_ref[0, r0:r0 + rows, :] = ycols


def _step_state(h0, xdt, dec, xc):
    m = h0.shape[0]
    bt = _STEP_BT
    assert m % bt == 0
    nb = m // bt
    cols = lambda a: a.reshape(nb, bt, D_INNER).transpose(0, 2, 1)
    h_new, y_cols = pl.pallas_call(
        _step_state_body,
        grid=(nb,),
        in_specs=[
            pl.BlockSpec((bt, D_INNER, D_STATE), lambda i: (i, 0, 0)),
            pl.BlockSpec((1, D_INNER, bt), lambda i: (i, 0, 0)),
            pl.BlockSpec((1, D_INNER, bt), lambda i: (i, 0, 0)),
            pl.BlockSpec((bt, 512), lambda i: (i, D_INNER // 512)),
            pl.BlockSpec((bt, 512), lambda i: (i, D_INNER // 512 + 1))],
        out_specs=[
            pl.BlockSpec((bt, D_INNER, D_STATE), lambda i: (i, 0, 0)),
            pl.BlockSpec((1, D_INNER, bt), lambda i: (i, 0, 0))],
        out_shape=[
            jax.ShapeDtypeStruct((m, D_INNER, D_STATE), F32),
            jax.ShapeDtypeStruct((nb, D_INNER, bt), F32)],
        compiler_params=_cparams(1),
    )(h0, cols(xdt), cols(dec), xc, xc)
    return h_new, y_cols.transpose(0, 2, 1).reshape(m, D_INNER)


def _step_post_body(y_ref, xc_ref, z_ref, dskip_ref, ng_ref, yn_ref):
    gw = D_INNER // SSM_GROUPS
    for g in range(SSM_GROUPS):
        gs = slice(g * gw, (g + 1) * gw)
        y = y_ref[:, gs] + dskip_ref[:, gs] * xc_ref[:, gs]
        y = y * _silu(z_ref[:, gs])
        ms = jnp.mean(y * y, axis=-1, keepdims=True)
        yn_ref[:, gs] = (y * lax.rsqrt(ms + NORM_EPS) * ng_ref[:, gs]).astype(BF16)


def _step_post(y, xc, proj, dskip_x, norm_g):
    m = y.shape[0]
    return pl.pallas_call(
        _step_post_body,
        grid=(1,),
        in_specs=[
            pl.BlockSpec((m, D_INNER), lambda i: (0, 0)),
            pl.BlockSpec((m, D_INNER), lambda i: (0, 0)),
            pl.BlockSpec((m, D_INNER), lambda i: (0, COL_Z // D_INNER)),
            pl.BlockSpec((1, D_INNER), lambda i: (0, 0)),
            pl.BlockSpec((1, D_INNER), lambda i: (0, 0))],
        out_specs=pl.BlockSpec((m, D_INNER), lambda i: (0, 0)),
        out_shape=jax.ShapeDtypeStruct((m, D_INNER), BF16),
        compiler_params=_cparams(1),
    )(y, xc, proj, dskip_x, norm_g)


def _step_attn_body(q_ref, kn_ref, vn_ref, kv0_ref, kv1_ref, kv2_ref,
                    o0_ref, l0_ref, o1_ref, l1_ref, o2_ref, l2_ref):
    st = ATTN_STEPS
    dh = ATTN_HEAD_DIM
    wq = ATTN_OUT_WIDTH
    jrow = lax.broadcasted_iota(jnp.int32, (1, st), 1)
    q_all = q_ref[0] * (ATTN_HEAD_DIM ** -0.5)
    kn_all = kn_ref[0]
    vn_all = vn_ref[0]
    for g, (kv_ref, o_ref, l_ref) in enumerate(((kv0_ref, o0_ref, l0_ref), (kv1_ref, o1_ref, l1_ref),
                                                (kv2_ref, o2_ref, l2_ref))):
        dil = DIL_GROUPS[g][1]
        dist = ((st - jrow) * dil).astype(F32)
        for h in range(HEADS_PER_DIL_GROUP):
            cs = slice(g * wq + h * dh, g * wq + (h + 1) * dh)
            qh = q_all[:, cs]
            kh = kv_ref[0, :, h * dh:(h + 1) * dh].astype(BF16)
            vh = kv_ref[0, :, wq + h * dh:wq + (h + 1) * dh].astype(BF16)
            q8 = jnp.broadcast_to(qh, (SUBLANES, dh)).astype(BF16)
            s = _dot_nt(q8, kh)[0:1, :] - _slope(g * HEADS_PER_DIL_GROUP + h) * dist
            s_new = jnp.sum(qh * kn_all[:, cs], axis=-1, keepdims=True)
            m = jnp.maximum(jnp.max(s, axis=-1, keepdims=True), s_new)
            p = jnp.exp(s - m)
            p_new = jnp.exp(s_new - m)
            l = jnp.sum(p, axis=-1, keepdims=True) + p_new
            p8 = jnp.broadcast_to(p, (SUBLANES, st)).astype(BF16)
            o = jnp.dot(p8, vh, preferred_element_type=F32)[0:1, :] + p_new * vn_all[:, cs]
            o_ref[0, :, h * dh:(h + 1) * dh] = o / l
            l_ref[0, :, h * dh:(h + 1) * dh] = jnp.broadcast_to(m + jnp.log(l), (1, dh))


def _step_attn(q, k_new, v_new, kv_sel):
    m = q.shape[0]
    wq = ATTN_OUT_WIDTH
    r3 = lambda a: a.reshape(m, 1, a.shape[-1])
    vec = pl.BlockSpec((1, 1, ATTN_WIDTH), lambda i: (i, 0, 0))
    kvs = pl.BlockSpec((1, ATTN_STEPS, 2 * wq), lambda i: (i, 0, 0))
    outs = pl.pallas_call(
        _step_attn_body,
        grid=(m,),
        in_specs=[vec, vec, vec, kvs, kvs, kvs],
        out_specs=[pl.BlockSpec((1, 1, wq), lambda i: (i, 0, 0))] * 6,
        out_shape=[jax.ShapeDtypeStruct((m, 1, wq), F32)] * 6,
        compiler_params=_cparams(1),
    )(r3(q), r3(k_new), r3(v_new), *kv_sel)
    return [o.reshape(m, wq) for o in outs]


def _prep_weights(w_in, conv_b, dt_bias, a_log, d_skip, ssm_norm_g, w_out_ssm, w_out_attn, w_out,
                  ln1_g, ln1_b, w_router, b_router, ln2_g, ln2_b):
    cuts = np.cumsum((D_INNER, CONV_DIM, SSM_HEADS, ATTN_WIDTH, ATTN_WIDTH, ATTN_WIDTH, D_MODEL, D_MODEL))
    z, xbc, dt, q, k, v, ga, gb = jnp.split(w_in, [int(c) for c in cuts[:-1]], axis=1)
    xs, bm, cm = jnp.split(xbc, [D_INNER, D_INNER + 512], axis=1)
    dt_pad = jnp.zeros((D_MODEL, PROJ_W - COL_DT - SSM_HEADS), w_in.dtype)
    w_perm = jnp.concatenate([z, xs, ga, gb, bm, cm, q, k, v, dt, dt_pad], axis=1).astype(BF16)
    pad_heads = lambda a: jnp.pad(a.astype(F32), (0, LANES - SSM_HEADS)).reshape(1, LANES)
    row = lambda a: a.astype(F32).reshape(1, -1)
    return dict(
        w_perm=w_perm, conv_b=row(conv_b), dtb=pad_heads(dt_bias), alog=pad_heads(a_log),
        dskip=row(jnp.repeat(d_skip, SSM_HEAD_DIM)), norm_g=row(ssm_norm_g),
        w_ssm=w_out_ssm.astype(BF16), w_attn=w_out_attn.astype(BF16), w_out=w_out.astype(BF16),
        g1=row(ln1_g), b1=row(ln1_b),
        w_r=jnp.pad(w_router, ((0, 0), (0, LANES - N_EXPERTS))).astype(BF16),
        b_r=jnp.pad(b_router.astype(F32), (0, LANES - N_EXPERTS)).reshape(1, LANES),
        g2=row(ln2_g), b2=row(ln2_b))


def _layer_prompt(x, p, conv_w, moe_w):
    bsz, t, _ = x.shape
    x2d = x.reshape(bsz * t, D_MODEL)
    proj = _in_proj(x2d, p['w_perm'])
    yn, h_new, conv_new = _ssd_prompt(proj, bsz, t, conv_w, p['conv_b'], p['dtb'], p['alog'],
                                      p['dskip'], p['norm_g'])
    attn, kvs = [], []
    for g, (window, _) in enumerate(DIL_GROUPS):
        o, lse, kw, vw = _attn_prompt(proj, bsz, t, g)
        attn.extend((o, lse))
        heads = lambda a: a.reshape(bsz, window, HEADS_PER_DIL_GROUP, ATTN_HEAD_DIM)
        kvs.append(jnp.stack([heads(kw), heads(vw)], axis=2))
    x1, x1p, idx_dense, gates_dense, rank_dense, counts = _mix(
        yn, attn, proj, x2d, p['w_ssm'], p['w_attn'], p['w_out'], p['g1'], p['b1'], p['w_r'], p['b_r'])
    y = _moe_and_norm(x1, x1p, idx_dense, gates_dense, rank_dense, counts, *moe_w, p['g2'], p['b2'])
    return y.reshape(bsz, t, D_MODEL), h_new, conv_new, kvs


def _layer_step(x, conv_buf, h0, kv_bufs, p, conv_w, moe_w):
    m = x.shape[0]
    x2d = x.reshape(m, D_MODEL)
    proj = _in_proj(x2d, p['w_perm'])
    xc, conv_new_t, xdt, dec = _step_pre(proj, conv_buf.transpose(1, 0, 2), conv_w, p['conv_b'],
                                         p['dtb'], p['alog'])
    h_new, y = _step_state(h0.reshape(m, D_INNER, D_STATE), xdt, dec, xc)
    yn = _step_post(y, xc, proj, p['dskip'], p['norm_g'])
    wq = ATTN_OUT_WIDTH
    kv_sel, kv_new = [], []
    for g, (window, dil) in enumerate(DIL_GROUPS):
        buf = kv_bufs[g]
        kv_sel.append(buf[:, ::dil].reshape(m, ATTN_STEPS, 2 * wq))
        k = proj[:, COL_K + g * wq:COL_K + (g + 1) * wq]
        v = proj[:, COL_V + g * wq:COL_V + (g + 1) * wq]
        new = jnp.stack([k, v], axis=1).reshape(m, 1, 2, HEADS_PER_DIL_GROUP, ATTN_HEAD_DIM)
        kv_new.append(jnp.concatenate([buf[:, 1:], new], axis=1))
    attn = _step_attn(proj[:, COL_Q:COL_Q + ATTN_WIDTH], proj[:, COL_K:COL_K + ATTN_WIDTH],
                      proj[:, COL_V:COL_V + ATTN_WIDTH], kv_sel)
    x1, x1p, idx_dense, gates_dense, rank_dense, counts = _mix(
        yn, attn, proj, x2d, p['w_ssm'], p['w_attn'], p['w_out'], p['g1'], p['b1'], p['w_r'], p['b_r'])
    y_out = _moe_and_norm(x1, x1p, idx_dense, gates_dense, rank_dense, counts, *moe_w, p['g2'],
                          p['b2'])
    h_new = h_new.reshape(m, SSM_HEADS, SSM_HEAD_DIM, D_STATE)
    return y_out.reshape(m, 1, D_MODEL), h_new, conv_new_t.transpose(1, 0, 2), kv_new


def kernel(x_prompt, x_sample, state_ssm, state_conv, cache_kv_w128, cache_kv_w512, cache_kv_w2048, w_in, conv_w, conv_b, dt_bias, a_log, d_skip, ssm_norm_g, w_out_ssm, w_out_attn, w_out, ln1_g, ln1_b, w_router, b_router, w_gate_up, b_gate_up, w_down, b_down, ln2_g, ln2_b):
    assert w_in.shape[0] == DEPTH == 1 and x_sample.shape[1] == 1
    l = 0
    p = _prep_weights(w_in[l], conv_b[l], dt_bias[l], a_log[l], d_skip[l], ssm_norm_g[l], w_out_ssm[l],
                      w_out_attn[l], w_out[l], ln1_g[l], ln1_b[l], w_router[l], b_router[l], ln2_g[l],
                      ln2_b[l])
    moe_w = (w_gate_up[l], b_gate_up[l], w_down[l], b_down[l])
    y_p, hp, cp, kvp = _layer_prompt(x_prompt, p, conv_w[l], moe_w)
    y_s, hs, cs, kvs = _layer_step(x_sample, state_conv[l], state_ssm[l],
                                   (cache_kv_w128[l], cache_kv_w512[l], cache_kv_w2048[l]), p,
                                   conv_w[l], moe_w)
    stack = lambda a: a[None]
    return (y_p, y_s, stack(hp), stack(cp), stack(kvp[0]), stack(kvp[1]), stack(kvp[2]),
            stack(hs), stack(cs), stack(kvs[0]), stack(kvs[1]), stack(kvs[2]))
```

```python
import functools

import numpy as np
import jax
import jax.numpy as jnp
from jax import lax
from jax.experimental import pallas as pl
from jax.experimental.pallas import tpu as pltpu

F32 = jnp.float32
BF16 = jnp.bfloat16

D_MODEL = 1024
D_INNER = 2048
SSM_HEAD_DIM = 64
SSM_HEADS = 32
SSM_GROUPS = 4
D_STATE = 128
CONV_WIDTH = 4
CONV_DIM = D_INNER + 2 * SSM_GROUPS * D_STATE
SSD_CHUNK = 128
ATTN_HEAD_DIM = 64
HEADS_PER_DIL_GROUP = 4
DIL_GROUPS = ((128, 1), (512, 4), (2048, 16))
N_ATTN_HEADS = HEADS_PER_DIL_GROUP * len(DIL_GROUPS)
ATTN_WIDTH = N_ATTN_HEADS * ATTN_HEAD_DIM
ATTN_OUT_WIDTH = HEADS_PER_DIL_GROUP * ATTN_HEAD_DIM
ATTN_STEPS = 128
N_EXPERTS = 32
TOP_K = 4
D_FF = D_MODEL
SWIGLU_LIMIT = 7.0
SWIGLU_ALPHA = 1.702
DEPTH = 1
ALPHA = (2.0 * DEPTH) ** 0.25
NORM_EPS = 1e-5

LANES = 128
SUBLANES = 8
VMEM_LIMIT = 56 * 1024 * 1024

COL_Z = 0
COL_XS = 2048
COL_GA = 4096
COL_GB = 5120
COL_B = 6144
COL_C = 6656
COL_Q = 7168
COL_K = 7936
COL_V = 8704
COL_DT = 9472
PROJ_W = 9728

MOE_ROWS = 256


def _cparams(n_grid):
    return pltpu.CompilerParams(dimension_semantics=("arbitrary",) * n_grid,
                                vmem_limit_bytes=VMEM_LIMIT)


def _sigmoid(x):
    return 1.0 / (1.0 + jnp.exp(-x))


def _silu(x):
    return x * _sigmoid(x)


def _softplus(x):
    return jnp.maximum(x, 0.0) + jnp.log1p(jnp.exp(-jnp.abs(x)))


def _layer_norm(v, g, b):
    mu = jnp.mean(v, axis=-1, keepdims=True)
    d = v - mu
    var = jnp.mean(d * d, axis=-1, keepdims=True)
    return d * lax.rsqrt(var + NORM_EPS) * g + b


def _dot_nt(a, b):
    return lax.dot_general(a, b, (((1,), (1,)), ((), ())), preferred_element_type=F32)


def _dot_tn(a, b):
    return lax.dot_general(a, b, (((0,), (0,)), ((), ())), preferred_element_type=F32)


def _dot_exact(a, b):
    return jnp.dot(a, b, preferred_element_type=F32, precision=lax.Precision.HIGHEST)


ROW_TILE = D_MODEL // LANES
assert ROW_TILE == SUBLANES


def _rows_to_tiles(x, tiles_ref, base=0):
    n = x.shape[0]
    for c in range(ROW_TILE):
        tiles_ref[pl.ds(base + c, n, stride=ROW_TILE), :] = x[:, c * LANES:(c + 1) * LANES]


def _tile_column(tiles_ref, n, c, base=0):
    return tiles_ref[pl.ds(base + c, n, stride=ROW_TILE), :]


def _proj_body(x_ref, w_ref, o_ref, xb_ref):
    @pl.when(pl.program_id(1) == 0)
    def _():
        xb_ref[...] = x_ref[...].astype(BF16)

    o_ref[...] = jnp.dot(xb_ref[...], w_ref[...], preferred_element_type=F32)


def _in_proj(x2d, w_perm):
    m = x2d.shape[0]
    tm = min(2048, m)
    tn = 512
    assert m % tm == 0 and PROJ_W % tn == 0
    return pl.pallas_call(
        _proj_body,
        grid=(m // tm, PROJ_W // tn),
        in_specs=[pl.BlockSpec((tm, D_MODEL), lambda i, j: (i, 0)),
                  pl.BlockSpec((D_MODEL, tn), lambda i, j: (0, j))],
        out_specs=pl.BlockSpec((tm, tn), lambda i, j: (i, j)),
        out_shape=jax.ShapeDtypeStruct((m, PROJ_W), F32),
        scratch_shapes=[pltpu.VMEM((tm, D_MODEL), BF16)],
        compiler_params=_cparams(2),
    )(x2d, w_perm)


_PAIR = 2 * SSM_HEAD_DIM
_N_PAIRS = SSM_HEADS // 2
_PAIRS_PER_GROUP = _N_PAIRS // SSM_GROUPS
_EXT_PAD = SUBLANES


def _ssd_body(z_ref, xs_ref, b_ref, c_ref, dt_ref, cw_ref, cb_ref, dtb_ref, alog_ref, dskip_ref,
              ng_ref, yn_ref, hout_ref, convout_ref, ext_ref, xc_ref, s_ref, ybuf_ref):
    c = pl.program_id(1)
    nc = pl.num_programs(1)
    lc = SSD_CHUNK

    @pl.when(c == 0)
    def _():
        ext_ref[0:_EXT_PAD, :] = jnp.zeros((_EXT_PAD, CONV_DIM), F32)
        s_ref[...] = jnp.zeros(s_ref.shape, F32)

    ext_ref[_EXT_PAD:_EXT_PAD + lc, 0:D_INNER] = xs_ref[...]
    ext_ref[_EXT_PAD:_EXT_PAD + lc, D_INNER:D_INNER + 512] = b_ref[...]
    ext_ref[_EXT_PAD:_EXT_PAD + lc, D_INNER + 512:CONV_DIM] = c_ref[...]

    cstep = 256
    for c0 in range(0, CONV_DIM, cstep):
        cs = slice(c0, c0 + cstep)
        acc = cb_ref[:, cs] + cw_ref[CONV_WIDTH - 1:CONV_WIDTH, cs] * ext_ref[_EXT_PAD:_EXT_PAD + lc, cs]
        for s in range(1, CONV_WIDTH):
            acc = acc + (cw_ref[CONV_WIDTH - 1 - s:CONV_WIDTH - s, cs]
                         * ext_ref[_EXT_PAD - s:_EXT_PAD - s + lc, cs])
        xc_ref[:, cs] = _silu(acc)

    @pl.when(c == nc - 1)
    def _():
        convout_ref[0] = ext_ref[_EXT_PAD + lc - (CONV_WIDTH - 1):_EXT_PAD + lc, :]

    ext_ref[0:_EXT_PAD, :] = ext_ref[lc:lc + _EXT_PAD, :]

    dtv = _softplus(dt_ref[...] + dtb_ref[...])
    a_neg = -jnp.exp(alog_ref[...])
    d_a = dtv * a_neg
    row = lax.broadcasted_iota(jnp.int32, (lc, lc), 0)
    col = lax.broadcasted_iota(jnp.int32, (lc, lc), 1)
    causal = row >= col
    tril = causal.astype(F32)
    a_cum = _dot_exact(tril, d_a)
    a_cum_t = a_cum.T
    a_last = a_cum[lc - 1:lc, :]
    chunk_decay = jnp.exp(a_last)
    decay_end = jnp.exp(a_last - a_cum)
    exp_a = jnp.exp(a_cum)
    dte = dtv * decay_end

    lane = lax.broadcasted_iota(jnp.int32, (lc, LANES), 1)
    first = lane < SSM_HEAD_DIM
    lane_row = lax.broadcasted_iota(jnp.int32, (1, LANES), 1)
    first_row = lane_row < SSM_HEAD_DIM

    def pick(arr, h0):
        return jnp.where(first, arr[:, h0:h0 + 1], arr[:, h0 + 1:h0 + 2])

    for g in range(SSM_GROUPS):
        bg = xc_ref[:, D_INNER + g * D_STATE:D_INNER + (g + 1) * D_STATE]
        cg = xc_ref[:, D_INNER + 512 + g * D_STATE:D_INNER + 512 + (g + 1) * D_STATE]
        bg_b = bg.astype(BF16)
        cb = _dot_nt(cg.astype(BF16), bg_b)
        for j in range(_PAIRS_PER_GROUP):
            pi = g * _PAIRS_PER_GROUP + j
            h0 = 2 * pi
            xs_pair = xc_ref[:, pi * _PAIR:(pi + 1) * _PAIR]
            xdt = xs_pair * pick(dtv, h0)
            xdte = xs_pair * pick(dte, h0)
            lhs = []
            for h in (h0, h0 + 1):
                seg = a_cum[:, h:h + 1] - a_cum_t[h:h + 1, :]
                lhs.append((cb * jnp.exp(jnp.where(causal, seg, -jnp.inf))).astype(BF16))
            for h in (h0, h0 + 1):
                lhs.append((cg * exp_a[:, h:h + 1]).astype(BF16))
            lhs = jnp.concatenate(lhs, axis=1)
            s_old = s_ref[pi]
            zero = jnp.zeros_like(xdt)
            rhs = jnp.concatenate([jnp.where(first, xdt, zero), jnp.where(first, zero, xdt),
                                   jnp.where(first, s_old, zero), jnp.where(first, zero, s_old)],
                                  axis=0).astype(BF16)
            ybuf_ref[:, pi * _PAIR:(pi + 1) * _PAIR] = jnp.dot(lhs, rhs, preferred_element_type=F32)
            cd = jnp.where(first_row, chunk_decay[:, h0:h0 + 1], chunk_decay[:, h0 + 1:h0 + 2])
            s_ref[pi] = cd * s_old + _dot_tn(bg_b, xdte.astype(BF16))

    gw = D_INNER // SSM_GROUPS
    for g in range(SSM_GROUPS):
        gs = slice(g * gw, (g + 1) * gw)
        y = ybuf_ref[:, gs] + dskip_ref[:, gs] * xc_ref[:, gs]
        y = y * _silu(z_ref[:, gs])
        ms = jnp.mean(y * y, axis=-1, keepdims=True)
        yn_ref[:, gs] = (y * lax.rsqrt(ms + NORM_EPS) * ng_ref[:, gs]).astype(BF16)

    @pl.when(c == nc - 1)
    def _():
        for pi in range(_N_PAIRS):
            t = s_ref[pi].T
            hout_ref[0, 2 * pi] = t[0:SSM_HEAD_DIM]
            hout_ref[0, 2 * pi + 1] = t[SSM_HEAD_DIM:_PAIR]


def _ssd_prompt(proj, bsz, t, conv_w, conv_b, dtb_pad, alog_pad, dskip_x, norm_g):
    lc = SSD_CHUNK
    assert t % lc == 0
    nc = t // lc
    row = lambda b, c: b * nc + c
    const = lambda b, c: (0, 0)
    return pl.pallas_call(
        _ssd_body,
        grid=(bsz, nc),
        in_specs=[
            pl.BlockSpec((lc, D_INNER), lambda b, c: (row(b, c), COL_Z // D_INNER)),
            pl.BlockSpec((lc, D_INNER), lambda b, c: (row(b, c), COL_XS // D_INNER)),
            pl.BlockSpec((lc, 512), lambda b, c: (row(b, c), COL_B // 512)),
            pl.BlockSpec((lc, 512), lambda b, c: (row(b, c), COL_C // 512)),
            pl.BlockSpec((lc, LANES), lambda b, c: (row(b, c), COL_DT // LANES)),
            pl.BlockSpec((CONV_WIDTH, CONV_DIM), const),
            pl.BlockSpec((1, CONV_DIM), const),
            pl.BlockSpec((1, LANES), const),
            pl.BlockSpec((1, LANES), const),
            pl.BlockSpec((1, D_INNER), const),
            pl.BlockSpec((1, D_INNER), const),
        ],
        out_specs=[
            pl.BlockSpec((lc, D_INNER), lambda b, c: (row(b, c), 0)),
            pl.BlockSpec((1, SSM_HEADS, SSM_HEAD_DIM, D_STATE), lambda b, c: (b, 0, 0, 0)),
            pl.BlockSpec((1, CONV_WIDTH - 1, CONV_DIM), lambda b, c: (b, 0, 0)),
        ],
        out_shape=[
            jax.ShapeDtypeStruct((bsz * t, D_INNER), BF16),
            jax.ShapeDtypeStruct((bsz, SSM_HEADS, SSM_HEAD_DIM, D_STATE), F32),
            jax.ShapeDtypeStruct((bsz, CONV_WIDTH - 1, CONV_DIM), F32),
        ],
        scratch_shapes=[
            pltpu.VMEM((lc + _EXT_PAD, CONV_DIM), F32),
            pltpu.VMEM((lc, CONV_DIM), F32),
            pltpu.VMEM((_N_PAIRS, D_STATE, _PAIR), F32),
            pltpu.VMEM((lc, D_INNER), F32),
        ],
        compiler_params=_cparams(2),
    )(proj, proj, proj, proj, proj, conv_w, conv_b, dtb_pad, alog_pad, dskip_x, norm_g)


def _slope(head):
    return float(np.float32(2.0 ** (-8.0 * (head + 1) / N_ATTN_HEADS)))


_HEADS_PER_SLAB = LANES // ATTN_HEAD_DIM
_SLABS_PER_GROUP = HEADS_PER_DIL_GROUP // _HEADS_PER_SLAB


def _attn_body(q_ref, k_ref, v_ref, o_ref, lse_ref, kw_ref, vw_ref, *, group, dil):
    slab = pl.program_id(1)
    st = ATTN_STEPS
    dh = ATTN_HEAD_DIM
    t = q_ref.shape[0]
    rb = st * dil
    kw_ref[...] = k_ref[t - rb:t, :]
    vw_ref[...] = v_ref[t - rb:t, :]

    def mask_and_dist(n_keys):
        i = lax.broadcasted_iota(jnp.int32, (st, n_keys), 0)
        j = lax.broadcasted_iota(jnp.int32, (st, n_keys), 1)
        d_sub = i + (n_keys - st) - j
        return (d_sub >= 0) & (d_sub <= st), (d_sub * dil).astype(F32)

    first = mask_and_dist(st)
    later = mask_and_dist(2 * st)

    def rows(start, n):
        return pl.ds(start, n, stride=dil) if dil > 1 else pl.ds(start, n)

    for blk in range(t // rb):
        valid, dist = first if blk == 0 else later
        n_keys = st if blk == 0 else 2 * st
        for r in range(dil):
            q_rows = rows(blk * rb + r, st)
            k_rows = rows(max(blk - 1, 0) * rb + r, n_keys)
            q = q_ref[q_rows, :] * (ATTN_HEAD_DIM ** -0.5)
            k2 = k_ref[k_rows, :].astype(BF16)
            v2 = v_ref[k_rows, :].astype(BF16)
            o_parts, lse_parts = [], []
            for h in range(_HEADS_PER_SLAB):
                hs = slice(h * dh, (h + 1) * dh)
                base = group * HEADS_PER_DIL_GROUP + h
                slope = jnp.where(slab == 0, _slope(base), _slope(base + _HEADS_PER_SLAB))
                s = _dot_nt(q[:, hs].astype(BF16), k2[:, hs]) - slope * dist
                s = jnp.where(valid, s, -jnp.inf)
                m = jnp.max(s, axis=-1, keepdims=True)
                p = jnp.exp(s - m)
                l = jnp.sum(p, axis=-1, keepdims=True)
                o_parts.append(jnp.dot(p.astype(BF16), v2[:, hs], preferred_element_type=F32) / l)
                lse_parts.append(jnp.broadcast_to(m + jnp.log(l), (st, dh)))
            o_ref[q_rows, :] = jnp.concatenate(o_parts, axis=1)
            lse_ref[q_rows, :] = jnp.concatenate(lse_parts, axis=1)


def _attn_prompt(proj, bsz, t, group):
    window, dil = DIL_GROUPS[group]
    rb = ATTN_STEPS * dil
    assert window == rb and t % rb == 0 and _SLABS_PER_GROUP == 2
    wq = ATTN_OUT_WIDTH
    col = lambda c0: c0 // LANES + group * _SLABS_PER_GROUP
    seq = lambda c0: pl.BlockSpec((t, LANES), lambda b, s: (b, col(c0) + s))
    return pl.pallas_call(
        functools.partial(_attn_body, group=group, dil=dil),
        grid=(bsz, _SLABS_PER_GROUP),
        in_specs=[seq(COL_Q), seq(COL_K), seq(COL_V)],
        out_specs=[pl.BlockSpec((t, LANES), lambda b, s: (b, s)),
                   pl.BlockSpec((t, LANES), lambda b, s: (b, s)),
                   pl.BlockSpec((rb, LANES), lambda b, s: (b, s)),
                   pl.BlockSpec((rb, LANES), lambda b, s: (b, s))],
        out_shape=[jax.ShapeDtypeStruct((bsz * t, wq), F32),
                   jax.ShapeDtypeStruct((bsz * t, wq), F32),
                   jax.ShapeDtypeStruct((bsz * window, wq), F32),
                   jax.ShapeDtypeStruct((bsz * window, wq), F32)],
        compiler_params=_cparams(2),
    )(proj, proj, proj)


def _mix_body(yn_ref, o0_ref, l0_ref, o1_ref, l1_ref, o2_ref, l2_ref, ga_ref, gb_ref, x_ref,
              wssm_ref, wattn_ref, wout_ref, g1_ref, b1_ref, wr_ref, br_ref,
              x1t_ref, idx_ref, gate_ref, rank_ref, cnt_ref, carry_ref):
    @pl.when(pl.program_id(0) == 0)
    def _():
        carry_ref[...] = jnp.zeros(carry_ref.shape, F32)

    branch_a = jnp.dot(yn_ref[...], wssm_ref[...], preferred_element_type=F32)
    l0, l1, l2 = l0_ref[...], l1_ref[...], l2_ref[...]
    m = jnp.maximum(jnp.maximum(l0, l1), l2)
    e0, e1, e2 = jnp.exp(l0 - m), jnp.exp(l1 - m), jnp.exp(l2 - m)
    o = (e0 * o0_ref[...] + e1 * o1_ref[...] + e2 * o2_ref[...]) / (e0 + e1 + e2)
    branch_b = jnp.dot(o.astype(BF16), wattn_ref[...], preferred_element_type=F32)
    merged = _sigmoid(ga_ref[...]) * branch_a + _sigmoid(gb_ref[...]) * branch_b
    mix = jnp.dot(merged.astype(BF16), wout_ref[...], preferred_element_type=F32)
    x1 = _layer_norm(ALPHA * x_ref[...] + mix, g1_ref[...], b1_ref[...])
    _rows_to_tiles(x1, x1t_ref)

    logits = jnp.dot(x1.astype(BF16), wr_ref[...], preferred_element_type=F32) + br_ref[...]
    lane = lax.broadcasted_iota(jnp.int32, logits.shape, 1)
    logits = jnp.where(lane < N_EXPERTS, logits, -jnp.inf)
    vals, idxs = [], []
    for _ in range(TOP_K):
        mk = jnp.max(logits, axis=-1, keepdims=True)
        ik = jnp.min(jnp.where(logits == mk, lane, LANES), axis=-1, keepdims=True)
        vals.append(mk)
        idxs.append(ik)
        logits = jnp.where(lane == ik, -jnp.inf, logits)
    es = [jnp.exp(v - vals[0]) for v in vals]
    den = es[0] + es[1] + es[2] + es[3]
    tm = lane.shape[0]
    onehot = jnp.zeros(lane.shape, F32)
    for k in range(TOP_K):
        onehot = onehot + (lane == idxs[k]).astype(F32)
    ri = lax.broadcasted_iota(jnp.int32, (tm, tm), 0)
    ci = lax.broadcasted_iota(jnp.int32, (tm, tm), 1)
    before = (ri > ci).astype(BF16)
    prefix = jnp.dot(before, onehot.astype(BF16), preferred_element_type=F32) + carry_ref[...]
    idx_out = jnp.zeros(lane.shape, jnp.int32)
    gate_out = jnp.zeros(lane.shape, F32)
    rank_out = jnp.zeros(lane.shape, jnp.int32)
    for k in range(TOP_K):
        rank_k = jnp.sum(jnp.where(lane == idxs[k], prefix, 0.0), axis=-1, keepdims=True)
        idx_out = jnp.where(lane == k, idxs[k], idx_out)
        gate_out = jnp.where(lane == k, es[k] / den, gate_out)
        rank_out = jnp.where(lane == k, rank_k.astype(jnp.int32), rank_out)
    idx_ref[...] = idx_out
    gate_ref[...] = gate_out
    rank_ref[...] = rank_out
    carry_ref[...] = carry_ref[...] + jnp.sum(onehot, axis=0, keepdims=True)
    cnt_ref[...] = carry_ref[...].astype(jnp.int32)


def _mix(yn, attn, proj, x2d, w_ssm, w_attn, w_out, g1, b1, w_r, b_r):
    m = x2d.shape[0]
    tm = min(256, m)
    assert m % tm == 0
    rowblk = lambda w: pl.BlockSpec((tm, w), lambda i: (i, 0))
    const = lambda a: pl.BlockSpec(a.shape, lambda i: (0,) * a.ndim)
    attn_specs = [rowblk(ATTN_OUT_WIDTH)] * 6
    return pl.pallas_call(
        _mix_body,
        grid=(m // tm,),
        in_specs=[rowblk(D_INNER)] + attn_specs + [
            pl.BlockSpec((tm, D_MODEL), lambda i: (i, COL_GA // D_MODEL)),
            pl.BlockSpec((tm, D_MODEL), lambda i: (i, COL_GB // D_MODEL)),
            rowblk(D_MODEL),
            const(w_ssm), const(w_attn), const(w_out), const(g1), const(b1), const(w_r), const(b_r)],
        out_specs=[pl.BlockSpec((tm * ROW_TILE, LANES), lambda i: (i, 0)),
                   rowblk(LANES), rowblk(LANES), rowblk(LANES),
                   pl.BlockSpec((1, LANES), lambda i: (0, 0))],
        out_shape=[jax.ShapeDtypeStruct((m * ROW_TILE, LANES), F32),
                   jax.ShapeDtypeStruct((m, LANES), jnp.int32),
                   jax.ShapeDtypeStruct((m, LANES), F32),
                   jax.ShapeDtypeStruct((m, LANES), jnp.int32),
                   jax.ShapeDtypeStruct((1, LANES), jnp.int32)],
        scratch_shapes=[pltpu.VMEM((1, LANES), F32)],
        compiler_params=_cparams(1),
    )(yn, *attn, proj, proj, x2d, w_ssm, w_attn, w_out, g1, b1, w_r, b_r)


def _tile_copy(idx_ref, r, src_hbm, dst, sem):
    t = pl.multiple_of(idx_ref[0, 0, r] * ROW_TILE, ROW_TILE)
    d = r * ROW_TILE if isinstance(r, int) else pl.multiple_of(r * ROW_TILE, ROW_TILE)
    return pltpu.make_async_copy(src_hbm.at[pl.ds(t, ROW_TILE)], dst.at[pl.ds(d, ROW_TILE)], sem)


def _row_gather(idx_ref, n_rows, src_hbm, dst, sem):
    def body(r, carry):
        _tile_copy(idx_ref, r, src_hbm, dst, sem).start()
        return carry
    lax.fori_loop(0, n_rows, body, 0, unroll=8)


def _row_gather_unrolled(idx_ref, n_rows, src_hbm, dst, sem):
    for r in range(n_rows):
        _tile_copy(idx_ref, r, src_hbm, dst, sem).start()


def _row_gather_wait(n_rows, src_hbm, dst, sem):
    pltpu.make_async_copy(src_hbm.at[pl.ds(0, n_rows * ROW_TILE)], dst, sem).wait()


def _expert_body(blk_exp_ref, n_used_ref, tok_cur_ref, tok_nxt_ref, x_hbm, wgu_ref, bgu_ref, wd_ref,
                 bd_ref, out_ref, xbuf, sem, wgu_b, wd_b):
    i = pl.program_id(0)
    n_used = n_used_ref[0]
    tb = MOE_ROWS
    parity = lax.rem(i, 2)

    @pl.when(i == 0)
    def _():
        _row_gather(tok_cur_ref, tb, x_hbm, xbuf.at[0], sem.at[0])

    e = blk_exp_ref[i]
    e_prev = blk_exp_ref[jnp.maximum(i - 1, 0)]

    @pl.when((i == 0) | ((e != e_prev) & (i < n_used)))
    def _():
        wgu_b[...] = wgu_ref[0].astype(BF16)
        wd_b[...] = wd_ref[0].astype(BF16)

    def block(slot):
        _row_gather_unrolled(tok_nxt_ref, tb, x_hbm, xbuf.at[1 - slot], sem.at[1 - slot])
        _row_gather_wait(tb, x_hbm, xbuf.at[slot], sem.at[slot])
        x = jnp.concatenate([_tile_column(xbuf.at[slot], tb, c).astype(BF16) for c in range(ROW_TILE)],
                            axis=1)
        h = jnp.dot(x, wgu_b[...], preferred_element_type=F32) + bgu_ref[0]
        gate = jnp.minimum(h[:, :D_FF], SWIGLU_LIMIT)
        up = jnp.clip(h[:, D_FF:], -SWIGLU_LIMIT, SWIGLU_LIMIT)
        hmid = (up + 1.0) * (gate * _sigmoid(SWIGLU_ALPHA * gate))
        _rows_to_tiles(jnp.dot(hmid.astype(BF16), wd_b[...], preferred_element_type=F32) + bd_ref[0],
                       out_ref)

    for slot in (0, 1):
        pl.when((i < n_used) & (parity == slot))(functools.partial(block, slot))

    @pl.when(i == n_used - 1)
    def _():
        _row_gather_wait(tb, x_hbm, xbuf.at[1 - parity], sem.at[1 - parity])

    @pl.when(i >= n_used)
    def _():
        out_ref[...] = jnp.zeros(out_ref.shape, F32)


def _experts(x1t, blk_exp, n_used, slot_tok, w_gate_up, b_gate_up, w_down, b_down):
    tb = MOE_ROWS
    n_blk = blk_exp.shape[0]
    tok3 = slot_tok.reshape(n_blk, 1, tb)
    grid_spec = pltpu.PrefetchScalarGridSpec(
        num_scalar_prefetch=2,
        grid=(n_blk,),
        in_specs=[
            pl.BlockSpec((1, 1, tb), lambda i, be, nu: (i, 0, 0), memory_space=pltpu.SMEM),
            pl.BlockSpec((1, 1, tb), lambda i, be, nu: (jnp.minimum(i + 1, n_blk - 1), 0, 0),
                         memory_space=pltpu.SMEM),
            pl.BlockSpec(memory_space=pl.ANY),
            pl.BlockSpec((1, D_MODEL, 2 * D_FF), lambda i, be, nu: (be[i], 0, 0)),
            pl.BlockSpec((1, 1, 2 * D_FF), lambda i, be, nu: (be[i], 0, 0)),
            pl.BlockSpec((1, D_FF, D_MODEL), lambda i, be, nu: (be[i], 0, 0)),
            pl.BlockSpec((1, 1, D_MODEL), lambda i, be, nu: (be[i], 0, 0)),
        ],
        out_specs=pl.BlockSpec((tb * ROW_TILE, LANES), lambda i, be, nu: (i, 0)),
        scratch_shapes=[
            pltpu.VMEM((2, tb * ROW_TILE, LANES), F32),
            pltpu.SemaphoreType.DMA((2,)),
            pltpu.VMEM((D_MODEL, 2 * D_FF), BF16),
            pltpu.VMEM((D_FF, D_MODEL), BF16),
        ],
    )
    return pl.pallas_call(
        _expert_body,
        grid_spec=grid_spec,
        out_shape=jax.ShapeDtypeStruct((n_blk * tb * ROW_TILE, LANES), F32),
        compiler_params=_cparams(1),
    )(blk_exp, n_used, tok3, tok3, x1t, w_gate_up, b_gate_up.reshape(N_EXPERTS, 1, 2 * D_FF),
      w_down, b_down.reshape(N_EXPERTS, 1, D_MODEL))


_COMBINE_TOKENS = 128


def _combine_body(dst_cur_ref, dst_nxt_ref, rows_hbm, gate_ref, x1t_ref, g2_ref, b2_ref, y_ref,
                  buf, sem, acc_ref):
    i = pl.program_id(0)
    n = pl.num_programs(0)
    tt = _COMBINE_TOKENS
    nr = TOP_K * tt
    parity = lax.rem(i, 2)

    @pl.when(i == 0)
    def _():
        _row_gather(dst_cur_ref, nr, rows_hbm, buf.at[0], sem.at[0])

    def tile(slot):
        _row_gather_unrolled(dst_nxt_ref, nr, rows_hbm, buf.at[1 - slot], sem.at[1 - slot])
        _row_gather_wait(nr, rows_hbm, buf.at[slot], sem.at[slot])
        gates = gate_ref[...]
        gk = [jnp.broadcast_to(gates[:, k:k + 1], (tt, LANES)) for k in range(TOP_K)]
        for c in range(ROW_TILE):
            acc = ALPHA * _tile_column(x1t_ref, tt, c)
            for k in range(TOP_K):
                acc = acc + gk[k] * _tile_column(buf.at[slot], tt, c, base=k * tt * ROW_TILE)
            acc_ref[:, c * LANES:(c + 1) * LANES] = acc
        y_ref[...] = _layer_norm(acc_ref[...], g2_ref[...], b2_ref[...])

    for slot in (0, 1):
        pl.when(parity == slot)(functools.partial(tile, slot))

    @pl.when(i == n - 1)
    def _():
        _row_gather_wait(nr, rows_hbm, buf.at[1 - parity], sem.at[1 - parity])


def _combine(rows, dest, gates_dense, x1t, g2, b2):
    m = x1t.shape[0] // ROW_TILE
    tt = _COMBINE_TOKENS
    assert m % tt == 0
    nt = m // tt
    dst3 = dest.reshape(nt, tt, TOP_K).transpose(0, 2, 1).reshape(nt, 1, TOP_K * tt)
    const = lambda a: pl.BlockSpec(a.shape, lambda i: (0,) * a.ndim)
    return pl.pallas_call(
        _combine_body,
        grid=(nt,),
        in_specs=[
            pl.BlockSpec((1, 1, TOP_K * tt), lambda i: (i, 0, 0), memory_space=pltpu.SMEM),
            pl.BlockSpec((1, 1, TOP_K * tt), lambda i: (jnp.minimum(i + 1, nt - 1), 0, 0),
                         memory_space=pltpu.SMEM),
            pl.BlockSpec(memory_space=pl.ANY),
            pl.BlockSpec((tt, LANES), lambda i: (i, 0)),
            pl.BlockSpec((tt * ROW_TILE, LANES), lambda i: (i, 0)),
            const(g2), const(b2),
        ],
        out_specs=pl.BlockSpec((tt, D_MODEL), lambda i: (i, 0)),
        out_shape=jax.ShapeDtypeStruct((m, D_MODEL), F32),
        scratch_shapes=[pltpu.VMEM((2, TOP_K * tt * ROW_TILE, LANES), F32), pltpu.SemaphoreType.DMA((2,)),
                        pltpu.VMEM((tt, D_MODEL), F32)],
        compiler_params=_cparams(1),
    )(dst3, dst3, rows, gates_dense, x1t, g2, b2)


def _route(top_idx, rank, counts):
    tb = MOE_ROWS
    n_tok = top_idx.shape[0]
    n_asg = n_tok * TOP_K
    start = jnp.cumsum(counts) - counts
    padded = (counts + tb - 1) // tb * tb
    pend = jnp.cumsum(padded)
    pstart = pend - padded
    n_blk = -(-n_asg // tb) + N_EXPERTS
    blk_lo = jnp.arange(n_blk, dtype=jnp.int32) * tb
    blk_exp = jnp.minimum(jnp.sum(blk_lo[:, None] >= pend[None, :], axis=1), N_EXPERTS - 1).astype(jnp.int32)
    n_used = (pend[N_EXPERTS - 1] // tb).astype(jnp.int32).reshape(1)
    order = jnp.argsort(top_idx.reshape(-1)).astype(jnp.int32)
    slot = jnp.arange(n_blk * tb, dtype=jnp.int32)
    slot_e = jnp.repeat(blk_exp, tb)
    within = slot - pstart[slot_e]
    src = jnp.clip(start[slot_e] + within, 0, n_asg - 1)
    slot_tok = jnp.where(within < counts[slot_e], order[src] // TOP_K, 0).astype(jnp.int32)
    dest = (pstart[top_idx] + rank).astype(jnp.int32)
    return blk_exp, n_used, slot_tok, dest


def _moe_and_norm(x1t, idx_dense, gates_dense, rank_dense, counts, w_gate_up, b_gate_up, w_down,
                  b_down, g2, b2):
    blk_exp, n_used, slot_tok, dest = _route(idx_dense[:, :TOP_K], rank_dense[:, :TOP_K],
                                             counts[0, :N_EXPERTS])
    rows = _experts(x1t, blk_exp, n_used, slot_tok, w_gate_up, b_gate_up, w_down, b_down)
    return _combine(rows, dest, gates_dense, x1t, g2, b2)


def _expand_heads(v):
    r = lax.broadcasted_iota(jnp.int32, (LANES, D_INNER), 0)
    c = lax.broadcasted_iota(jnp.int32, (LANES, D_INNER), 1)
    sel = (c // SSM_HEAD_DIM == r).astype(F32)
    return _dot_exact(v, sel)


def _step_pre_body(xs_ref, b_ref, c_ref, dt_ref, conv_ref, cw_ref, cb_ref, dtb_ref, alog_ref,
                   xc_ref, convout_ref, xdt_ref, dec_ref):
    w = CONV_WIDTH
    for (c0, c1, src) in ((0, D_INNER, xs_ref), (D_INNER, D_INNER + 512, b_ref),
                          (D_INNER + 512, CONV_DIM, c_ref)):
        cs = slice(c0, c1)
        new = src[...]
        acc = cb_ref[:, cs] + cw_ref[w - 1:w, cs] * new
        for s in range(w - 1):
            acc = acc + cw_ref[s:s + 1, cs] * conv_ref[s, :, cs]
        xc_ref[:, cs] = _silu(acc)
        for s in range(w - 2):
            convout_ref[s, :, cs] = conv_ref[s + 1, :, cs]
        convout_ref[w - 2, :, cs] = new
    dtv = _softplus(dt_ref[...] + dtb_ref[...])
    dec = jnp.exp(dtv * (-jnp.exp(alog_ref[...])))
    xdt_ref[...] = xc_ref[:, 0:D_INNER] * _expand_heads(dtv)
    dec_ref[...] = _expand_heads(dec)


def _step_pre(proj, conv_t, conv_w, conv_b, dtb_pad, alog_pad):
    m = proj.shape[0]
    const2 = lambda a: pl.BlockSpec(a.shape, lambda i: (0,) * a.ndim)
    return pl.pallas_call(
        _step_pre_body,
        grid=(1,),
        in_specs=[
            pl.BlockSpec((m, D_INNER), lambda i: (0, COL_XS // D_INNER)),
            pl.BlockSpec((m, 512), lambda i: (0, COL_B // 512)),
            pl.BlockSpec((m, 512), lambda i: (0, COL_C // 512)),
            pl.BlockSpec((m, LANES), lambda i: (0, COL_DT // LANES)),
            const2(conv_t), const2(conv_w), const2(conv_b), const2(dtb_pad), const2(alog_pad)],
        out_specs=[
            pl.BlockSpec((m, CONV_DIM), lambda i: (0, 0)),
            pl.BlockSpec((CONV_WIDTH - 1, m, CONV_DIM), lambda i: (0, 0, 0)),
            pl.BlockSpec((m, D_INNER), lambda i: (0, 0)),
            pl.BlockSpec((m, D_INNER), lambda i: (0, 0))],
        out_shape=[
            jax.ShapeDtypeStruct((m, CONV_DIM), F32),
            jax.ShapeDtypeStruct((CONV_WIDTH - 1, m, CONV_DIM), F32),
            jax.ShapeDtypeStruct((m, D_INNER), F32),
            jax.ShapeDtypeStruct((m, D_INNER), F32)],
        compiler_params=_cparams(1),
    )(proj, proj, proj, proj, conv_t, conv_w, conv_b, dtb_pad, alog_pad)


_STEP_BT = 8


def _step_state_body(h_ref, xdt_ref, dec_ref, b_ref, c_ref, hout_ref, y_ref):
    rows = 128
    per_group = D_INNER // SSM_GROUPS
    lane = lax.broadcasted_iota(jnp.int32, (rows, _STEP_BT), 1)
    for r0 in range(0, D_INNER, rows):
        g = r0 // per_group
        ycols = jnp.zeros((rows, _STEP_BT), F32)
        for jb in range(_STEP_BT):
            brow = b_ref[jb:jb + 1, g * D_STATE:(g + 1) * D_STATE]
            crow = c_ref[jb:jb + 1, g * D_STATE:(g + 1) * D_STATE]
            xcol = xdt_ref[0, r0:r0 + rows, jb:jb + 1]
            dcol = dec_ref[0, r0:r0 + rows, jb:jb + 1]
            hn = dcol * h_ref[jb, r0:r0 + rows, :] + xcol * brow
            hout_ref[jb, r0:r0 + rows, :] = hn
            ycols = jnp.where(lane == jb, jnp.sum(hn * crow, axis=-1, keepdims=True), ycols)
        y_ref[0, r0:r0 + rows, :] = ycols


def _step_state(h0, xdt, dec, xc):
    m = h0.shape[0]
    bt = _STEP_BT
    assert m % bt == 0
    nb = m // bt
    cols = lambda a: a.reshape(nb, bt, D_INNER).transpose(0, 2, 1)
    h_new, y_cols = pl.pallas_call(
        _step_state_body,
        grid=(nb,),
        in_specs=[
            pl.BlockSpec((bt, D_INNER, D_STATE), lambda i: (i, 0, 0)),
            pl.BlockSpec((1, D_INNER, bt), lambda i: (i, 0, 0)),
            pl.BlockSpec((1, D_INNER, bt), lambda i: (i, 0, 0)),
            pl.BlockSpec((bt, 512), lambda i: (i, D_INNER // 512)),
            pl.BlockSpec((bt, 512), lambda i: (i, D_INNER // 512 + 1))],
        out_specs=[
            pl.BlockSpec((bt, D_INNER, D_STATE), lambda i: (i, 0, 0)),
            pl.BlockSpec((1, D_INNER, bt), lambda i: (i, 0, 0))],
        out_shape=[
            jax.ShapeDtypeStruct((m, D_INNER, D_STATE), F32),
            jax.ShapeDtypeStruct((nb, D_INNER, bt), F32)],
        compiler_params=_cparams(1),
    )(h0, cols(xdt), cols(dec), xc, xc)
    return h_new, y_cols.transpose(0, 2, 1).reshape(m, D_INNER)


def _step_post_body(y_ref, xc_ref, z_ref, dskip_ref, ng_ref, yn_ref):
    gw = D_INNER // SSM_GROUPS
    for g in range(SSM_GROUPS):
        gs = slice(g * gw, (g + 1) * gw)
        y = y_ref[:, gs] + dskip_ref[:, gs] * xc_ref[:, gs]
        y = y * _silu(z_ref[:, gs])
        ms = jnp.mean(y * y, axis=-1, keepdims=True)
        yn_ref[:, gs] = (y * lax.rsqrt(ms + NORM_EPS) * ng_ref[:, gs]).astype(BF16)


def _step_post(y, xc, proj, dskip_x, norm_g):
    m = y.shape[0]
    return pl.pallas_call(
        _step_post_body,
        grid=(1,),
        in_specs=[
            pl.BlockSpec((m, D_INNER), lambda i: (0, 0)),
            pl.BlockSpec((m, D_INNER), lambda i: (0, 0)),
            pl.BlockSpec((m, D_INNER), lambda i: (0, COL_Z // D_INNER)),
            pl.BlockSpec((1, D_INNER), lambda i: (0, 0)),
            pl.BlockSpec((1, D_INNER), lambda i: (0, 0))],
        out_specs=pl.BlockSpec((m, D_INNER), lambda i: (0, 0)),
        out_shape=jax.ShapeDtypeStruct((m, D_INNER), BF16),
        compiler_params=_cparams(1),
    )(y, xc, proj, dskip_x, norm_g)


def _step_attn_body(q_ref, kn_ref, vn_ref, kv0_ref, kv1_ref, kv2_ref,
                    o0_ref, l0_ref, o1_ref, l1_ref, o2_ref, l2_ref):
    st = ATTN_STEPS
    dh = ATTN_HEAD_DIM
    wq = ATTN_OUT_WIDTH
    jrow = lax.broadcasted_iota(jnp.int32, (1, st), 1)
    for jb in range(_STEP_BT):
        q_all = q_ref[jb] * (ATTN_HEAD_DIM ** -0.5)
        kn_all = kn_ref[jb]
        vn_all = vn_ref[jb]
        for g, (kv_ref, o_ref, l_ref) in enumerate(((kv0_ref, o0_ref, l0_ref), (kv1_ref, o1_ref, l1_ref),
                                                    (kv2_ref, o2_ref, l2_ref))):
            dil = DIL_GROUPS[g][1]
            dist = ((st - jrow) * dil).astype(F32)
            for h in range(HEADS_PER_DIL_GROUP):
                cs = slice(g * wq + h * dh, g * wq + (h + 1) * dh)
                qh = q_all[:, cs]
                kh = kv_ref[jb, :, h * dh:(h + 1) * dh].astype(BF16)
                vh = kv_ref[jb, :, wq + h * dh:wq + (h + 1) * dh].astype(BF16)
                q8 = jnp.broadcast_to(qh, (SUBLANES, dh)).astype(BF16)
                s = _dot_nt(q8, kh)[0:1, :] - _slope(g * HEADS_PER_DIL_GROUP + h) * dist
                s_new = jnp.sum(qh * kn_all[:, cs], axis=-1, keepdims=True)
                m = jnp.maximum(jnp.max(s, axis=-1, keepdims=True), s_new)
                p = jnp.exp(s - m)
                p_new = jnp.exp(s_new - m)
                l = jnp.sum(p, axis=-1, keepdims=True) + p_new
                p8 = jnp.broadcast_to(p, (SUBLANES, st)).astype(BF16)
                o = jnp.dot(p8, vh, preferred_element_type=F32)[0:1, :] + p_new * vn_all[:, cs]
                o_ref[jb, :, h * dh:(h + 1) * dh] = o / l
                l_ref[jb, :, h * dh:(h + 1) * dh] = jnp.broadcast_to(m + jnp.log(l), (1, dh))


def _step_attn(q, k_new, v_new, kv_sel):
    m = q.shape[0]
    bt = _STEP_BT
    assert m % bt == 0
    wq = ATTN_OUT_WIDTH
    r3 = lambda a: a.reshape(m, 1, a.shape[-1])
    vec = pl.BlockSpec((bt, 1, ATTN_WIDTH), lambda i: (i, 0, 0))
    kvs = pl.BlockSpec((bt, ATTN_STEPS, 2 * wq), lambda i: (i, 0, 0))
    outs = pl.pallas_call(
        _step_attn_body,
        grid=(m // bt,),
        in_specs=[vec, vec, vec, kvs, kvs, kvs],
        out_specs=[pl.BlockSpec((bt, 1, wq), lambda i: (i, 0, 0))] * 6,
        out_shape=[jax.ShapeDtypeStruct((m, 1, wq), F32)] * 6,
        compiler_params=_cparams(1),
    )(r3(q), r3(k_new), r3(v_new), *kv_sel)
    return [o.reshape(m, wq) for o in outs]


def _take_rows_body(src_ref, dst_ref):
    dst_ref[...] = src_ref[...]


def _dilated_rows(buf, dil):
    if dil == 1:
        return buf
    m, window = buf.shape[:2]
    tail = buf.shape[2:]
    st = window // dil
    bt = 4
    assert m % bt == 0 and window % dil == 0
    zeros = (0,) * len(tail)
    return pl.pallas_call(
        _take_rows_body,
        grid=(m // bt,),
        in_specs=[pl.BlockSpec((bt, st, None) + tail, lambda i: (i, 0, 0) + zeros)],
        out_specs=pl.BlockSpec((bt, st) + tail, lambda i: (i, 0) + zeros),
        out_shape=jax.ShapeDtypeStruct((m, st) + tail, buf.dtype),
        compiler_params=_cparams(1),
    )(buf.reshape((m, st, dil) + tail))


def _prep_weights(w_in, conv_b, dt_bias, a_log, d_skip, ssm_norm_g, w_out_ssm, w_out_attn, w_out,
                  ln1_g, ln1_b, w_router, b_router, ln2_g, ln2_b):
    cuts = np.cumsum((D_INNER, CONV_DIM, SSM_HEADS, ATTN_WIDTH, ATTN_WIDTH, ATTN_WIDTH, D_MODEL, D_MODEL))
    z, xbc, dt, q, k, v, ga, gb = jnp.split(w_in, [int(c) for c in cuts[:-1]], axis=1)
    xs, bm, cm = jnp.split(xbc, [D_INNER, D_INNER + 512], axis=1)
    dt_pad = jnp.zeros((D_MODEL, PROJ_W - COL_DT - SSM_HEADS), w_in.dtype)
    w_perm = jnp.concatenate([z, xs, ga, gb, bm, cm, q, k, v, dt, dt_pad], axis=1).astype(BF16)
    pad_heads = lambda a: jnp.pad(a.astype(F32), (0, LANES - SSM_HEADS)).reshape(1, LANES)
    row = lambda a: a.astype(F32).reshape(1, -1)
    return dict(
        w_perm=w_perm, conv_b=row(conv_b), dtb=pad_heads(dt_bias), alog=pad_heads(a_log),
        dskip=row(jnp.repeat(d_skip, SSM_HEAD_DIM)), norm_g=row(ssm_norm_g),
        w_ssm=w_out_ssm.astype(BF16), w_attn=w_out_attn.astype(BF16), w_out=w_out.astype(BF16),
        g1=row(ln1_g), b1=row(ln1_b),
        w_r=jnp.pad(w_router, ((0, 0), (0, LANES - N_EXPERTS))).astype(BF16),
        b_r=jnp.pad(b_router.astype(F32), (0, LANES - N_EXPERTS)).reshape(1, LANES),
        g2=row(ln2_g), b2=row(ln2_b))


def _layer_prompt(x, p, conv_w, moe_w):
    bsz, t, _ = x.shape
    x2d = x.reshape(bsz * t, D_MODEL)
    proj = _in_proj(x2d, p['w_perm'])
    yn, h_new, conv_new = _ssd_prompt(proj, bsz, t, conv_w, p['conv_b'], p['dtb'], p['alog'],
                                      p['dskip'], p['norm_g'])
    attn, kvs = [], []
    for g, (window, _) in enumerate(DIL_GROUPS):
        o, lse, kw, vw = _attn_prompt(proj, bsz, t, g)
        attn.extend((o, lse))
        heads = lambda a: a.reshape(bsz, window, HEADS_PER_DIL_GROUP, ATTN_HEAD_DIM)
        kvs.append(jnp.stack([heads(kw), heads(vw)], axis=2))
    x1t, idx_dense, gates_dense, rank_dense, counts = _mix(
        yn, attn, proj, x2d, p['w_ssm'], p['w_attn'], p['w_out'], p['g1'], p['b1'], p['w_r'], p['b_r'])
    y = _moe_and_norm(x1t, idx_dense, gates_dense, rank_dense, counts, *moe_w, p['g2'], p['b2'])
    return y.reshape(bsz, t, D_MODEL), h_new, conv_new, kvs


def _layer_step(x, conv_buf, h0, kv_bufs, p, conv_w, moe_w):
    m = x.shape[0]
    x2d = x.reshape(m, D_MODEL)
    proj = _in_proj(x2d, p['w_perm'])
    xc, conv_new_t, xdt, dec = _step_pre(proj, conv_buf.transpose(1, 0, 2), conv_w, p['conv_b'],
                                         p['dtb'], p['alog'])
    h_new, y = _step_state(h0.reshape(m, D_INNER, D_STATE), xdt, dec, xc)
    yn = _step_post(y, xc, proj, p['dskip'], p['norm_g'])
    wq = ATTN_OUT_WIDTH
    kv_sel, kv_new = [], []
    for g, (window, dil) in enumerate(DIL_GROUPS):
        buf = kv_bufs[g]
        kv_sel.append(_dilated_rows(buf, dil).reshape(m, ATTN_STEPS, 2 * wq))
        k = proj[:, COL_K + g * wq:COL_K + (g + 1) * wq]
        v = proj[:, COL_V + g * wq:COL_V + (g + 1) * wq]
        new = jnp.stack([k, v], axis=1).reshape(m, 1, 2, HEADS_PER_DIL_GROUP, ATTN_HEAD_DIM)
        kv_new.append(jnp.concatenate([buf[:, 1:], new], axis=1))
    attn = _step_attn(proj[:, COL_Q:COL_Q + ATTN_WIDTH], proj[:, COL_K:COL_K + ATTN_WIDTH],
                      proj[:, COL_V:COL_V + ATTN_WIDTH], kv_sel)
    x1t, idx_dense, gates_dense, rank_dense, counts = _mix(
        yn, attn, proj, x2d, p['w_ssm'], p['w_attn'], p['w_out'], p['g1'], p['b1'], p['w_r'], p['b_r'])
    y_out = _moe_and_norm(x1t, idx_dense, gates_dense, rank_dense, counts, *moe_w, p['g2'], p['b2'])
    h_new = h_new.reshape(m, SSM_HEADS, SSM_HEAD_DIM, D_STATE)
    return y_out.reshape(m, 1, D_MODEL), h_new, conv_new_t.transpose(1, 0, 2), kv_new


def kernel(x_prompt, x_sample, state_ssm, state_conv, cache_kv_w128, cache_kv_w512, cache_kv_w2048, w_in, conv_w, conv_b, dt_bias, a_log, d_skip, ssm_norm_g, w_out_ssm, w_out_attn, w_out, ln1_g, ln1_b, w_router, b_router, w_gate_up, b_gate_up, w_down, b_down, ln2_g, ln2_b):
    assert w_in.shape[0] == DEPTH == 1 and x_sample.shape[1] == 1
    l = 0
    p = _prep_weights(w_in[l], conv_b[l], dt_bias[l], a_log[l], d_skip[l], ssm_norm_g[l], w_out_ssm[l],
                      w_out_attn[l], w_out[l], ln1_g[l], ln1_b[l], w_router[l], b_router[l], ln2_g[l],
                      ln2_b[l])
    moe_w = (w_gate_up[l], b_gate_up[l], w_down[l], b_down[l])
    y_p, hp, cp, kvp = _layer_prompt(x_prompt, p, conv_w[l], moe_w)
    y_s, hs, cs, kvs = _layer_step(x_sample, state_conv[l], state_ssm[l],
                                   (cache_kv_w128[l], cache_kv_w512[l], cache_kv_w2048[l]), p,
                                   conv_w[l], moe_w)
    stack = lambda a: a[None]
    return (y_p, y_s, stack(hp), stack(cp), stack(kvp[0]), stack(kvp[1]), stack(kvp[2]),
            stack(hs), stack(cs), stack(kvs[0]), stack(kvs[1]), stack(kvs[2]))
```

```python
import functools

import numpy as np
import jax
import jax.numpy as jnp
from jax import lax
from jax.experimental import pallas as pl
from jax.experimental.pallas import tpu as pltpu

F32 = jnp.float32
BF16 = jnp.bfloat16

D_MODEL = 1024
D_INNER = 2048
SSM_HEAD_DIM = 64
SSM_HEADS = 32
SSM_GROUPS = 4
D_STATE = 128
CONV_WIDTH = 4
CONV_DIM = D_INNER + 2 * SSM_GROUPS * D_STATE
SSD_CHUNK = 128
ATTN_HEAD_DIM = 64
HEADS_PER_DIL_GROUP = 4
DIL_GROUPS = ((128, 1), (512, 4), (2048, 16))
N_ATTN_HEADS = HEADS_PER_DIL_GROUP * len(DIL_GROUPS)
ATTN_WIDTH = N_ATTN_HEADS * ATTN_HEAD_DIM
ATTN_OUT_WIDTH = HEADS_PER_DIL_GROUP * ATTN_HEAD_DIM
ATTN_STEPS = 128
N_EXPERTS = 32
TOP_K = 4
D_FF = D_MODEL
SWIGLU_LIMIT = 7.0
SWIGLU_ALPHA = 1.702
DEPTH = 1
ALPHA = (2.0 * DEPTH) ** 0.25
NORM_EPS = 1e-5

LANES = 128
SUBLANES = 8
VMEM_LIMIT = 56 * 1024 * 1024

COL_Z = 0
COL_XS = 2048
COL_GA = 4096
COL_GB = 5120
COL_B = 6144
COL_C = 6656
COL_Q = 7168
COL_K = 7936
COL_V = 8704
COL_DT = 9472
PROJ_W = 9728

MOE_ROWS = 256


def _cparams(n_grid):
    return pltpu.CompilerParams(dimension_semantics=("arbitrary",) * n_grid,
                                vmem_limit_bytes=VMEM_LIMIT)


def _sigmoid(x):
    return 1.0 / (1.0 + jnp.exp(-x))


def _silu(x):
    return x * _sigmoid(x)


def _softplus(x):
    return jnp.maximum(x, 0.0) + jnp.log1p(jnp.exp(-jnp.abs(x)))


def _layer_norm(v, g, b):
    mu = jnp.mean(v, axis=-1, keepdims=True)
    d = v - mu
    var = jnp.mean(d * d, axis=-1, keepdims=True)
    return d * lax.rsqrt(var + NORM_EPS) * g + b


def _dot_nt(a, b):
    return lax.dot_general(a, b, (((1,), (1,)), ((), ())), preferred_element_type=F32)


def _dot_tn(a, b):
    return lax.dot_general(a, b, (((0,), (0,)), ((), ())), preferred_element_type=F32)


def _dot_exact(a, b):
    return jnp.dot(a, b, preferred_element_type=F32, precision=lax.Precision.HIGHEST)


ROW_TILE = D_MODEL // LANES
assert ROW_TILE == SUBLANES


def _rows_to_tiles(x, tiles_ref, base=0):
    n = x.shape[0]
    for c in range(ROW_TILE):
        tiles_ref[pl.ds(base + c, n, stride=ROW_TILE), :] = x[:, c * LANES:(c + 1) * LANES]


def _tile_column(tiles_ref, n, c, base=0):
    return tiles_ref[pl.ds(base + c, n, stride=ROW_TILE), :]


def _proj_body(x_ref, w_ref, o_ref, xb_ref):
    @pl.when(pl.program_id(1) == 0)
    def _():
        xb_ref[...] = x_ref[...].astype(BF16)

    o_ref[...] = jnp.dot(xb_ref[...], w_ref[...], preferred_element_type=F32)


def _in_proj(x2d, w_perm):
    m = x2d.shape[0]
    tm = min(2048, m)
    tn = 512
    assert m % tm == 0 and PROJ_W % tn == 0
    return pl.pallas_call(
        _proj_body,
        grid=(m // tm, PROJ_W // tn),
        in_specs=[pl.BlockSpec((tm, D_MODEL), lambda i, j: (i, 0)),
                  pl.BlockSpec((D_MODEL, tn), lambda i, j: (0, j))],
        out_specs=pl.BlockSpec((tm, tn), lambda i, j: (i, j)),
        out_shape=jax.ShapeDtypeStruct((m, PROJ_W), F32),
        scratch_shapes=[pltpu.VMEM((tm, D_MODEL), BF16)],
        compiler_params=_cparams(2),
    )(x2d, w_perm)


_PAIR = 2 * SSM_HEAD_DIM
_N_PAIRS = SSM_HEADS // 2
_PAIRS_PER_GROUP = _N_PAIRS // SSM_GROUPS
_EXT_PAD = SUBLANES


def _ssd_body(z_ref, xs_ref, b_ref, c_ref, dt_ref, cw_ref, cb_ref, dtb_ref, alog_ref, dskip_ref,
              ng_ref, yn_ref, hout_ref, convout_ref, ext_ref, xc_ref, s_ref, ybuf_ref):
    c = pl.program_id(1)
    nc = pl.num_programs(1)
    lc = SSD_CHUNK

    @pl.when(c == 0)
    def _():
        ext_ref[0:_EXT_PAD, :] = jnp.zeros((_EXT_PAD, CONV_DIM), F32)
        s_ref[...] = jnp.zeros(s_ref.shape, F32)

    ext_ref[_EXT_PAD:_EXT_PAD + lc, 0:D_INNER] = xs_ref[...]
    ext_ref[_EXT_PAD:_EXT_PAD + lc, D_INNER:D_INNER + 512] = b_ref[...]
    ext_ref[_EXT_PAD:_EXT_PAD + lc, D_INNER + 512:CONV_DIM] = c_ref[...]

    cstep = 256
    for c0 in range(0, CONV_DIM, cstep):
        cs = slice(c0, c0 + cstep)
        acc = cb_ref[:, cs] + cw_ref[CONV_WIDTH - 1:CONV_WIDTH, cs] * ext_ref[_EXT_PAD:_EXT_PAD + lc, cs]
        for s in range(1, CONV_WIDTH):
            acc = acc + (cw_ref[CONV_WIDTH - 1 - s:CONV_WIDTH - s, cs]
                         * ext_ref[_EXT_PAD - s:_EXT_PAD - s + lc, cs])
        xc_ref[:, cs] = _silu(acc)

    @pl.when(c == nc - 1)
    def _():
        convout_ref[0] = ext_ref[_EXT_PAD + lc - (CONV_WIDTH - 1):_EXT_PAD + lc, :]

    ext_ref[0:_EXT_PAD, :] = ext_ref[lc:lc + _EXT_PAD, :]

    dtv = _softplus(dt_ref[...] + dtb_ref[...])
    a_neg = -jnp.exp(alog_ref[...])
    d_a = dtv * a_neg
    row = lax.broadcasted_iota(jnp.int32, (lc, lc), 0)
    col = lax.broadcasted_iota(jnp.int32, (lc, lc), 1)
    causal = row >= col
    tril = causal.astype(F32)
    a_cum = _dot_exact(tril, d_a)
    a_cum_t = a_cum.T
    a_last = a_cum[lc - 1:lc, :]
    chunk_decay = jnp.exp(a_last)
    decay_end = jnp.exp(a_last - a_cum)
    exp_a = jnp.exp(a_cum)
    dte = dtv * decay_end

    lane = lax.broadcasted_iota(jnp.int32, (lc, LANES), 1)
    first = lane < SSM_HEAD_DIM
    lane_row = lax.broadcasted_iota(jnp.int32, (1, LANES), 1)
    first_row = lane_row < SSM_HEAD_DIM

    def pick(arr, h0):
        return jnp.where(first, arr[:, h0:h0 + 1], arr[:, h0 + 1:h0 + 2])

    for g in range(SSM_GROUPS):
        bg = xc_ref[:, D_INNER + g * D_STATE:D_INNER + (g + 1) * D_STATE]
        cg = xc_ref[:, D_INNER + 512 + g * D_STATE:D_INNER + 512 + (g + 1) * D_STATE]
        bg_b = bg.astype(BF16)
        cb = _dot_nt(cg.astype(BF16), bg_b)
        for j in range(_PAIRS_PER_GROUP):
            pi = g * _PAIRS_PER_GROUP + j
            h0 = 2 * pi
            xs_pair = xc_ref[:, pi * _PAIR:(pi + 1) * _PAIR]
            xdt = xs_pair * pick(dtv, h0)
            xdte = xs_pair * pick(dte, h0)
            lhs = []
            for h in (h0, h0 + 1):
                seg = a_cum[:, h:h + 1] - a_cum_t[h:h + 1, :]
                lhs.append((cb * jnp.exp(jnp.where(causal, seg, -jnp.inf))).astype(BF16))
            for h in (h0, h0 + 1):
                lhs.append((cg * exp_a[:, h:h + 1]).astype(BF16))
            lhs = jnp.concatenate(lhs, axis=1)
            s_old = s_ref[pi]
            zero = jnp.zeros_like(xdt)
            rhs = jnp.concatenate([jnp.where(first, xdt, zero), jnp.where(first, zero, xdt),
                                   jnp.where(first, s_old, zero), jnp.where(first, zero, s_old)],
                                  axis=0).astype(BF16)
            ybuf_ref[:, pi * _PAIR:(pi + 1) * _PAIR] = jnp.dot(lhs, rhs, preferred_element_type=F32)
            cd = jnp.where(first_row, chunk_decay[:, h0:h0 + 1], chunk_decay[:, h0 + 1:h0 + 2])
            s_ref[pi] = cd * s_old + _dot_tn(bg_b, xdte.astype(BF16))

    gw = D_INNER // SSM_GROUPS
    for g in range(SSM_GROUPS):
        gs = slice(g * gw, (g + 1) * gw)
        y = ybuf_ref[:, gs] + dskip_ref[:, gs] * xc_ref[:, gs]
        y = y * _silu(z_ref[:, gs])
        ms = jnp.mean(y * y, axis=-1, keepdims=True)
        yn_ref[:, gs] = (y * lax.rsqrt(ms + NORM_EPS) * ng_ref[:, gs]).astype(BF16)

    @pl.when(c == nc - 1)
    def _():
        for pi in range(_N_PAIRS):
            t = s_ref[pi].T
            hout_ref[0, 2 * pi] = t[0:SSM_HEAD_DIM]
            hout_ref[0, 2 * pi + 1] = t[SSM_HEAD_DIM:_PAIR]


def _ssd_prompt(proj, bsz, t, conv_w, conv_b, dtb_pad, alog_pad, dskip_x, norm_g):
    lc = SSD_CHUNK
    assert t % lc == 0
    nc = t // lc
    row = lambda b, c: b * nc + c
    const = lambda b, c: (0, 0)
    return pl.pallas_call(
        _ssd_body,
        grid=(bsz, nc),
        in_specs=[
            pl.BlockSpec((lc, D_INNER), lambda b, c: (row(b, c), COL_Z // D_INNER)),
            pl.BlockSpec((lc, D_INNER), lambda b, c: (row(b, c), COL_XS // D_INNER)),
            pl.BlockSpec((lc, 512), lambda b, c: (row(b, c), COL_B // 512)),
            pl.BlockSpec((lc, 512), lambda b, c: (row(b, c), COL_C // 512)),
            pl.BlockSpec((lc, LANES), lambda b, c: (row(b, c), COL_DT // LANES)),
            pl.BlockSpec((CONV_WIDTH, CONV_DIM), const),
            pl.BlockSpec((1, CONV_DIM), const),
            pl.BlockSpec((1, LANES), const),
            pl.BlockSpec((1, LANES), const),
            pl.BlockSpec((1, D_INNER), const),
            pl.BlockSpec((1, D_INNER), const),
        ],
        out_specs=[
            pl.BlockSpec((lc, D_INNER), lambda b, c: (row(b, c), 0)),
            pl.BlockSpec((1, SSM_HEADS, SSM_HEAD_DIM, D_STATE), lambda b, c: (b, 0, 0, 0)),
            pl.BlockSpec((1, CONV_WIDTH - 1, CONV_DIM), lambda b, c: (b, 0, 0)),
        ],
        out_shape=[
            jax.ShapeDtypeStruct((bsz * t, D_INNER), BF16),
            jax.ShapeDtypeStruct((bsz, SSM_HEADS, SSM_HEAD_DIM, D_STATE), F32),
            jax.ShapeDtypeStruct((bsz, CONV_WIDTH - 1, CONV_DIM), F32),
        ],
        scratch_shapes=[
            pltpu.VMEM((lc + _EXT_PAD, CONV_DIM), F32),
            pltpu.VMEM((lc, CONV_DIM), F32),
            pltpu.VMEM((_N_PAIRS, D_STATE, _PAIR), F32),
            pltpu.VMEM((lc, D_INNER), F32),
        ],
        compiler_params=_cparams(2),
    )(proj, proj, proj, proj, proj, conv_w, conv_b, dtb_pad, alog_pad, dskip_x, norm_g)


def _slope(head):
    return float(np.float32(2.0 ** (-8.0 * (head + 1) / N_ATTN_HEADS)))


_HEADS_PER_SLAB = LANES // ATTN_HEAD_DIM
_SLABS_PER_GROUP = HEADS_PER_DIL_GROUP // _HEADS_PER_SLAB


def _attn_body(q_ref, k_ref, v_ref, o_ref, lse_ref, kvt_ref, *, group, dil):
    slab = pl.program_id(1)
    st = ATTN_STEPS
    dh = ATTN_HEAD_DIM
    t = q_ref.shape[0]
    rb = st * dil
    kvt_ref[0, 0] = k_ref[t - rb:t, :].T
    kvt_ref[0, 1] = v_ref[t - rb:t, :].T

    def mask_and_dist(n_keys):
        i = lax.broadcasted_iota(jnp.int32, (st, n_keys), 0)
        j = lax.broadcasted_iota(jnp.int32, (st, n_keys), 1)
        d_sub = i + (n_keys - st) - j
        return (d_sub >= 0) & (d_sub <= st), (d_sub * dil).astype(F32)

    first = mask_and_dist(st)
    later = mask_and_dist(2 * st)

    def rows(start, n):
        return pl.ds(start, n, stride=dil) if dil > 1 else pl.ds(start, n)

    for blk in range(t // rb):
        valid, dist = first if blk == 0 else later
        n_keys = st if blk == 0 else 2 * st
        for r in range(dil):
            q_rows = rows(blk * rb + r, st)
            k_rows = rows(max(blk - 1, 0) * rb + r, n_keys)
            q = q_ref[q_rows, :] * (ATTN_HEAD_DIM ** -0.5)
            k2 = k_ref[k_rows, :].astype(BF16)
            v2 = v_ref[k_rows, :].astype(BF16)
            o_parts, lse_parts = [], []
            for h in range(_HEADS_PER_SLAB):
                hs = slice(h * dh, (h + 1) * dh)
                base = group * HEADS_PER_DIL_GROUP + h
                slope = jnp.where(slab == 0, _slope(base), _slope(base + _HEADS_PER_SLAB))
                s = _dot_nt(q[:, hs].astype(BF16), k2[:, hs]) - slope * dist
                s = jnp.where(valid, s, -jnp.inf)
                m = jnp.max(s, axis=-1, keepdims=True)
                p = jnp.exp(s - m)
                l = jnp.sum(p, axis=-1, keepdims=True)
                o_parts.append(jnp.dot(p.astype(BF16), v2[:, hs], preferred_element_type=F32) / l)
                lse_parts.append(jnp.broadcast_to(m + jnp.log(l), (st, dh)))
            o_ref[q_rows, :] = jnp.concatenate(o_parts, axis=1)
            lse_ref[q_rows, :] = jnp.concatenate(lse_parts, axis=1)


def _attn_prompt(proj, bsz, t, group):
    window, dil = DIL_GROUPS[group]
    rb = ATTN_STEPS * dil
    assert window == rb and t % rb == 0 and _SLABS_PER_GROUP == 2
    wq = ATTN_OUT_WIDTH
    col = lambda c0: c0 // LANES + group * _SLABS_PER_GROUP
    seq = lambda c0: pl.BlockSpec((t, LANES), lambda b, s: (b, col(c0) + s))
    return pl.pallas_call(
        functools.partial(_attn_body, group=group, dil=dil),
        grid=(bsz, _SLABS_PER_GROUP),
        in_specs=[seq(COL_Q), seq(COL_K), seq(COL_V)],
        out_specs=[pl.BlockSpec((t, LANES), lambda b, s: (b, s)),
                   pl.BlockSpec((t, LANES), lambda b, s: (b, s)),
                   pl.BlockSpec((1, 2, LANES, window), lambda b, s: (b, 0, s, 0))],
        out_shape=[jax.ShapeDtypeStruct((bsz * t, wq), F32),
                   jax.ShapeDtypeStruct((bsz * t, wq), F32),
                   jax.ShapeDtypeStruct((bsz, 2, wq, window), F32)],
        compiler_params=_cparams(2),
    )(proj, proj, proj)


def _window_from_lanes(kvt):
    bsz, _, _, window = kvt.shape
    return kvt.reshape(bsz, 2, HEADS_PER_DIL_GROUP, ATTN_HEAD_DIM, window).transpose(0, 4, 1, 2, 3)


def _window_to_lanes(buf):
    bsz, window = buf.shape[:2]
    return buf.transpose(0, 2, 3, 4, 1).reshape(bsz, 2, ATTN_OUT_WIDTH, window)


def _mix_body(yn_ref, o0_ref, l0_ref, o1_ref, l1_ref, o2_ref, l2_ref, ga_ref, gb_ref, x_ref,
              wssm_ref, wattn_ref, wout_ref, g1_ref, b1_ref, wr_ref, br_ref,
              x1t_ref, idx_ref, gate_ref, rank_ref, cnt_ref, carry_ref):
    @pl.when(pl.program_id(0) == 0)
    def _():
        carry_ref[...] = jnp.zeros(carry_ref.shape, F32)

    branch_a = jnp.dot(yn_ref[...], wssm_ref[...], preferred_element_type=F32)
    l0, l1, l2 = l0_ref[...], l1_ref[...], l2_ref[...]
    m = jnp.maximum(jnp.maximum(l0, l1), l2)
    e0, e1, e2 = jnp.exp(l0 - m), jnp.exp(l1 - m), jnp.exp(l2 - m)
    o = (e0 * o0_ref[...] + e1 * o1_ref[...] + e2 * o2_ref[...]) / (e0 + e1 + e2)
    branch_b = jnp.dot(o.astype(BF16), wattn_ref[...], preferred_element_type=F32)
    merged = _sigmoid(ga_ref[...]) * branch_a + _sigmoid(gb_ref[...]) * branch_b
    mix = jnp.dot(merged.astype(BF16), wout_ref[...], preferred_element_type=F32)
    x1 = _layer_norm(ALPHA * x_ref[...] + mix, g1_ref[...], b1_ref[...])
    _rows_to_tiles(x1, x1t_ref)

    logits = jnp.dot(x1.astype(BF16), wr_ref[...], preferred_element_type=F32) + br_ref[...]
    lane = lax.broadcasted_iota(jnp.int32, logits.shape, 1)
    logits = jnp.where(lane < N_EXPERTS, logits, -jnp.inf)
    vals, idxs = [], []
    for _ in range(TOP_K):
        mk = jnp.max(logits, axis=-1, keepdims=True)
        ik = jnp.min(jnp.where(logits == mk, lane, LANES), axis=-1, keepdims=True)
        vals.append(mk)
        idxs.append(ik)
        logits = jnp.where(lane == ik, -jnp.inf, logits)
    es = [jnp.exp(v - vals[0]) for v in vals]
    den = es[0] + es[1] + es[2] + es[3]
    tm = lane.shape[0]
    onehot = jnp.zeros(lane.shape, F32)
    for k in range(TOP_K):
        onehot = onehot + (lane == idxs[k]).astype(F32)
    ri = lax.broadcasted_iota(jnp.int32, (tm, tm), 0)
    ci = lax.broadcasted_iota(jnp.int32, (tm, tm), 1)
    before = (ri > ci).astype(BF16)
    prefix = jnp.dot(before, onehot.astype(BF16), preferred_element_type=F32) + carry_ref[...]
    idx_out = jnp.zeros(lane.shape, jnp.int32)
    gate_out = jnp.zeros(lane.shape, F32)
    rank_out = jnp.zeros(lane.shape, jnp.int32)
    for k in range(TOP_K):
        rank_k = jnp.sum(jnp.where(lane == idxs[k], prefix, 0.0), axis=-1, keepdims=True)
        idx_out = jnp.where(lane == k, idxs[k], idx_out)
        gate_out = jnp.where(lane == k, es[k] / den, gate_out)
        rank_out = jnp.where(lane == k, rank_k.astype(jnp.int32), rank_out)
    idx_ref[...] = idx_out
    gate_ref[...] = gate_out
    rank_ref[...] = rank_out
    carry_ref[...] = carry_ref[...] + jnp.sum(onehot, axis=0, keepdims=True)
    cnt_ref[...] = carry_ref[...].astype(jnp.int32)


def _mix(yn, attn, proj, x2d, w_ssm, w_attn, w_out, g1, b1, w_r, b_r):
    m = x2d.shape[0]
    tm = min(256, m)
    assert m % tm == 0
    rowblk = lambda w: pl.BlockSpec((tm, w), lambda i: (i, 0))
    const = lambda a: pl.BlockSpec(a.shape, lambda i: (0,) * a.ndim)
    attn_specs = [rowblk(ATTN_OUT_WIDTH)] * 6
    return pl.pallas_call(
        _mix_body,
        grid=(m // tm,),
        in_specs=[rowblk(D_INNER)] + attn_specs + [
            pl.BlockSpec((tm, D_MODEL), lambda i: (i, COL_GA // D_MODEL)),
            pl.BlockSpec((tm, D_MODEL), lambda i: (i, COL_GB // D_MODEL)),
            rowblk(D_MODEL),
            const(w_ssm), const(w_attn), const(w_out), const(g1), const(b1), const(w_r), const(b_r)],
        out_specs=[pl.BlockSpec((tm * ROW_TILE, LANES), lambda i: (i, 0)),
                   rowblk(LANES), rowblk(LANES), rowblk(LANES),
                   pl.BlockSpec((1, LANES), lambda i: (0, 0))],
        out_shape=[jax.ShapeDtypeStruct((m * ROW_TILE, LANES), F32),
                   jax.ShapeDtypeStruct((m, LANES), jnp.int32),
                   jax.ShapeDtypeStruct((m, LANES), F32),
                   jax.ShapeDtypeStruct((m, LANES), jnp.int32),
                   jax.ShapeDtypeStruct((1, LANES), jnp.int32)],
        scratch_shapes=[pltpu.VMEM((1, LANES), F32)],
        compiler_params=_cparams(1),
    )(yn, *attn, proj, proj, x2d, w_ssm, w_attn, w_out, g1, b1, w_r, b_r)


def _tile_copy(idx_ref, r, src_hbm, dst, sem):
    t = pl.multiple_of(idx_ref[0, 0, r] * ROW_TILE, ROW_TILE)
    d = r * ROW_TILE if isinstance(r, int) else pl.multiple_of(r * ROW_TILE, ROW_TILE)
    return pltpu.make_async_copy(src_hbm.at[pl.ds(t, ROW_TILE)], dst.at[pl.ds(d, ROW_TILE)], sem)


def _row_gather(idx_ref, n_rows, src_hbm, dst, sem):
    def body(r, carry):
        _tile_copy(idx_ref, r, src_hbm, dst, sem).start()
        return carry
    lax.fori_loop(0, n_rows, body, 0, unroll=8)


def _row_gather_unrolled(idx_ref, n_rows, src_hbm, dst, sem):
    for r in range(n_rows):
        _tile_copy(idx_ref, r, src_hbm, dst, sem).start()


def _row_gather_wait(n_rows, src_hbm, dst, sem):
    pltpu.make_async_copy(src_hbm.at[pl.ds(0, n_rows * ROW_TILE)], dst, sem).wait()


def _expert_body(blk_exp_ref, n_used_ref, tok_cur_ref, tok_nxt_ref, x_hbm, wgu_ref, bgu_ref, wd_ref,
                 bd_ref, out_ref, xbuf, sem, wgu_b, wd_b):
    i = pl.program_id(0)
    n_used = n_used_ref[0]
    tb = MOE_ROWS
    parity = lax.rem(i, 2)

    @pl.when(i == 0)
    def _():
        _row_gather(tok_cur_ref, tb, x_hbm, xbuf.at[0], sem.at[0])

    e = blk_exp_ref[i]
    e_prev = blk_exp_ref[jnp.maximum(i - 1, 0)]

    @pl.when((i == 0) | ((e != e_prev) & (i < n_used)))
    def _():
        wgu_b[...] = wgu_ref[0].astype(BF16)
        wd_b[...] = wd_ref[0].astype(BF16)

    def block(slot):
        _row_gather_unrolled(tok_nxt_ref, tb, x_hbm, xbuf.at[1 - slot], sem.at[1 - slot])
        _row_gather_wait(tb, x_hbm, xbuf.at[slot], sem.at[slot])
        x = jnp.concatenate([_tile_column(xbuf.at[slot], tb, c).astype(BF16) for c in range(ROW_TILE)],
                            axis=1)
        h = jnp.dot(x, wgu_b[...], preferred_element_type=F32) + bgu_ref[0]
        gate = jnp.minimum(h[:, :D_FF], SWIGLU_LIMIT)
        up = jnp.clip(h[:, D_FF:], -SWIGLU_LIMIT, SWIGLU_LIMIT)
        hmid = (up + 1.0) * (gate * _sigmoid(SWIGLU_ALPHA * gate))
        _rows_to_tiles(jnp.dot(hmid.astype(BF16), wd_b[...], preferred_element_type=F32) + bd_ref[0],
                       out_ref)

    for slot in (0, 1):
        pl.when((i < n_used) & (parity == slot))(functools.partial(block, slot))

    @pl.when(i == n_used - 1)
    def _():
        _row_gather_wait(tb, x_hbm, xbuf.at[1 - parity], sem.at[1 - parity])

    @pl.when(i >= n_used)
    def _():
        out_ref[...] = jnp.zeros(out_ref.shape, F32)


def _experts(x1t, blk_exp, n_used, slot_tok, w_gate_up, b_gate_up, w_down, b_down):
    tb = MOE_ROWS
    n_blk = blk_exp.shape[0]
    tok3 = slot_tok.reshape(n_blk, 1, tb)
    grid_spec = pltpu.PrefetchScalarGridSpec(
        num_scalar_prefetch=2,
        grid=(n_blk,),
        in_specs=[
            pl.BlockSpec((1, 1, tb), lambda i, be, nu: (i, 0, 0), memory_space=pltpu.SMEM),
            pl.BlockSpec((1, 1, tb), lambda i, be, nu: (jnp.minimum(i + 1, n_blk - 1), 0, 0),
                         memory_space=pltpu.SMEM),
            pl.BlockSpec(memory_space=pl.ANY),
            pl.BlockSpec((1, D_MODEL, 2 * D_FF), lambda i, be, nu: (be[i], 0, 0)),
            pl.BlockSpec((1, 1, 2 * D_FF), lambda i, be, nu: (be[i], 0, 0)),
            pl.BlockSpec((1, D_FF, D_MODEL), lambda i, be, nu: (be[i], 0, 0)),
            pl.BlockSpec((1, 1, D_MODEL), lambda i, be, nu: (be[i], 0, 0)),
        ],
        out_specs=pl.BlockSpec((tb * ROW_TILE, LANES), lambda i, be, nu: (i, 0)),
        scratch_shapes=[
            pltpu.VMEM((2, tb * ROW_TILE, LANES), F32),
            pltpu.SemaphoreType.DMA((2,)),
            pltpu.VMEM((D_MODEL, 2 * D_FF), BF16),
            pltpu.VMEM((D_FF, D_MODEL), BF16),
        ],
    )
    return pl.pallas_call(
        _expert_body,
        grid_spec=grid_spec,
        out_shape=jax.ShapeDtypeStruct((n_blk * tb * ROW_TILE, LANES), F32),
        compiler_params=_cparams(1),
    )(blk_exp, n_used, tok3, tok3, x1t, w_gate_up, b_gate_up.reshape(N_EXPERTS, 1, 2 * D_FF),
      w_down, b_down.reshape(N_EXPERTS, 1, D_MODEL))


_COMBINE_TOKENS = 128


def _combine_body(dst_cur_ref, dst_nxt_ref, rows_hbm, gate_ref, x1t_ref, g2_ref, b2_ref, y_ref,
                  buf, sem, acc_ref):
    i = pl.program_id(0)
    n = pl.num_programs(0)
    tt = _COMBINE_TOKENS
    nr = TOP_K * tt
    parity = lax.rem(i, 2)

    @pl.when(i == 0)
    def _():
        _row_gather(dst_cur_ref, nr, rows_hbm, buf.at[0], sem.at[0])

    def tile(slot):
        _row_gather_unrolled(dst_nxt_ref, nr, rows_hbm, buf.at[1 - slot], sem.at[1 - slot])
        _row_gather_wait(nr, rows_hbm, buf.at[slot], sem.at[slot])
        gates = gate_ref[...]
        gk = [jnp.broadcast_to(gates[:, k:k + 1], (tt, LANES)) for k in range(TOP_K)]
        for c in range(ROW_TILE):
            acc = ALPHA * _tile_column(x1t_ref, tt, c)
            for k in range(TOP_K):
                acc = acc + gk[k] * _tile_column(buf.at[slot], tt, c, base=k * tt * ROW_TILE)
            acc_ref[:, c * LANES:(c + 1) * LANES] = acc
        y_ref[...] = _layer_norm(acc_ref[...], g2_ref[...], b2_ref[...])

    for slot in (0, 1):
        pl.when(parity == slot)(functools.partial(tile, slot))

    @pl.when(i == n - 1)
    def _():
        _row_gather_wait(nr, rows_hbm, buf.at[1 - parity], sem.at[1 - parity])


def _combine(rows, dest, gates_dense, x1t, g2, b2):
    m = x1t.shape[0] // ROW_TILE
    tt = _COMBINE_TOKENS
    assert m % tt == 0
    nt = m // tt
    dst3 = dest.reshape(nt, tt, TOP_K).transpose(0, 2, 1).reshape(nt, 1, TOP_K * tt)
    const = lambda a: pl.BlockSpec(a.shape, lambda i: (0,) * a.ndim)
    return pl.pallas_call(
        _combine_body,
        grid=(nt,),
        in_specs=[
            pl.BlockSpec((1, 1, TOP_K * tt), lambda i: (i, 0, 0), memory_space=pltpu.SMEM),
            pl.BlockSpec((1, 1, TOP_K * tt), lambda i: (jnp.minimum(i + 1, nt - 1), 0, 0),
                         memory_space=pltpu.SMEM),
            pl.BlockSpec(memory_space=pl.ANY),
            pl.BlockSpec((tt, LANES), lambda i: (i, 0)),
            pl.BlockSpec((tt * ROW_TILE, LANES), lambda i: (i, 0)),
            const(g2), const(b2),
        ],
        out_specs=pl.BlockSpec((tt, D_MODEL), lambda i: (i, 0)),
        out_shape=jax.ShapeDtypeStruct((m, D_MODEL), F32),
        scratch_shapes=[pltpu.VMEM((2, TOP_K * tt * ROW_TILE, LANES), F32), pltpu.SemaphoreType.DMA((2,)),
                        pltpu.VMEM((tt, D_MODEL), F32)],
        compiler_params=_cparams(1),
    )(dst3, dst3, rows, gates_dense, x1t, g2, b2)


def _route(top_idx, rank, counts):
    tb = MOE_ROWS
    n_tok = top_idx.shape[0]
    n_asg = n_tok * TOP_K
    start = jnp.cumsum(counts) - counts
    padded = (counts + tb - 1) // tb * tb
    pend = jnp.cumsum(padded)
    pstart = pend - padded
    n_blk = -(-n_asg // tb) + N_EXPERTS
    blk_lo = jnp.arange(n_blk, dtype=jnp.int32) * tb
    blk_exp = jnp.minimum(jnp.sum(blk_lo[:, None] >= pend[None, :], axis=1), N_EXPERTS - 1).astype(jnp.int32)
    n_used = (pend[N_EXPERTS - 1] // tb).astype(jnp.int32).reshape(1)
    order = jnp.argsort(top_idx.reshape(-1)).astype(jnp.int32)
    slot = jnp.arange(n_blk * tb, dtype=jnp.int32)
    slot_e = jnp.repeat(blk_exp, tb)
    within = slot - pstart[slot_e]
    src = jnp.clip(start[slot_e] + within, 0, n_asg - 1)
    slot_tok = jnp.where(within < counts[slot_e], order[src] // TOP_K, 0).astype(jnp.int32)
    dest = (pstart[top_idx] + rank).astype(jnp.int32)
    return blk_exp, n_used, slot_tok, dest


def _moe_and_norm(x1t, idx_dense, gates_dense, rank_dense, counts, w_gate_up, b_gate_up, w_down,
                  b_down, g2, b2):
    blk_exp, n_used, slot_tok, dest = _route(idx_dense[:, :TOP_K], rank_dense[:, :TOP_K],
                                             counts[0, :N_EXPERTS])
    rows = _experts(x1t, blk_exp, n_used, slot_tok, w_gate_up, b_gate_up, w_down, b_down)
    return _combine(rows, dest, gates_dense, x1t, g2, b2)


def _expand_heads(v):
    r = lax.broadcasted_iota(jnp.int32, (LANES, D_INNER), 0)
    c = lax.broadcasted_iota(jnp.int32, (LANES, D_INNER), 1)
    sel = (c // SSM_HEAD_DIM == r).astype(F32)
    return _dot_exact(v, sel)


def _step_pre_body(xs_ref, b_ref, c_ref, dt_ref, conv_ref, cw_ref, cb_ref, dtb_ref, alog_ref,
                   xc_ref, convout_ref, xdt_ref, dec_ref):
    w = CONV_WIDTH
    for (c0, c1, src) in ((0, D_INNER, xs_ref), (D_INNER, D_INNER + 512, b_ref),
                          (D_INNER + 512, CONV_DIM, c_ref)):
        cs = slice(c0, c1)
        new = src[...]
        acc = cb_ref[:, cs] + cw_ref[w - 1:w, cs] * new
        for s in range(w - 1):
            acc = acc + cw_ref[s:s + 1, cs] * conv_ref[s, :, cs]
        xc_ref[:, cs] = _silu(acc)
        for s in range(w - 2):
            convout_ref[s, :, cs] = conv_ref[s + 1, :, cs]
        convout_ref[w - 2, :, cs] = new
    dtv = _softplus(dt_ref[...] + dtb_ref[...])
    dec = jnp.exp(dtv * (-jnp.exp(alog_ref[...])))
    xdt_ref[...] = xc_ref[:, 0:D_INNER] * _expand_heads(dtv)
    dec_ref[...] = _expand_heads(dec)


def _step_pre(proj, conv_t, conv_w, conv_b, dtb_pad, alog_pad):
    m = proj.shape[0]
    const2 = lambda a: pl.BlockSpec(a.shape, lambda i: (0,) * a.ndim)
    return pl.pallas_call(
        _step_pre_body,
        grid=(1,),
        in_specs=[
            pl.BlockSpec((m, D_INNER), lambda i: (0, COL_XS // D_INNER)),
            pl.BlockSpec((m, 512), lambda i: (0, COL_B // 512)),
            pl.BlockSpec((m, 512), lambda i: (0, COL_C // 512)),
            pl.BlockSpec((m, LANES), lambda i: (0, COL_DT // LANES)),
            const2(conv_t), const2(conv_w), const2(conv_b), const2(dtb_pad), const2(alog_pad)],
        out_specs=[
            pl.BlockSpec((m, CONV_DIM), lambda i: (0, 0)),
            pl.BlockSpec((CONV_WIDTH - 1, m, CONV_DIM), lambda i: (0, 0, 0)),
            pl.BlockSpec((m, D_INNER), lambda i: (0, 0)),
            pl.BlockSpec((m, D_INNER), lambda i: (0, 0))],
        out_shape=[
            jax.ShapeDtypeStruct((m, CONV_DIM), F32),
            jax.ShapeDtypeStruct((CONV_WIDTH - 1, m, CONV_DIM), F32),
            jax.ShapeDtypeStruct((m, D_INNER), F32),
            jax.ShapeDtypeStruct((m, D_INNER), F32)],
        compiler_params=_cparams(1),
    )(proj, proj, proj, proj, conv_t, conv_w, conv_b, dtb_pad, alog_pad)


_STEP_BT = 8


def _step_state_body(h_ref, xdt_ref, dec_ref, b_ref, c_ref, hout_ref, y_ref):
    rows = 128
    per_group = D_INNER // SSM_GROUPS
    lane = lax.broadcasted_iota(jnp.int32, (rows, _STEP_BT), 1)
    for r0 in range(0, D_INNER, rows):
        g = r0 // per_group
        ycols = jnp.zeros((rows, _STEP_BT), F32)
        for jb in range(_STEP_BT):
            brow = b_ref[jb:jb + 1, g * D_STATE:(g + 1) * D_STATE]
            crow = c_ref[jb:jb + 1, g * D_STATE:(g + 1) * D_STATE]
            xcol = xdt_ref[0, r0:r0 + rows, jb:jb + 1]
            dcol = dec_ref[0, r0:r0 + rows, jb:jb + 1]
            hn = dcol * h_ref[jb, r0:r0 + rows, :] + xcol * brow
            hout_ref[jb, r0:r0 + rows, :] = hn
            ycols = jnp.where(lane == jb, jnp.sum(hn * crow, axis=-1, keepdims=True), ycols)
        y_ref[0, r0:r0 + rows, :] = ycols


def _step_state(h0, xdt, dec, xc):
    m = h0.shape[0]
    bt = _STEP_BT
    assert m % bt == 0
    nb = m // bt
    cols = lambda a: a.reshape(nb, bt, D_INNER).transpose(0, 2, 1)
    h_new, y_cols = pl.pallas_call(
        _step_state_body,
        grid=(nb,),
        in_specs=[
            pl.BlockSpec((bt, D_INNER, D_STATE), lambda i: (i, 0, 0)),
            pl.BlockSpec((1, D_INNER, bt), lambda i: (i, 0, 0)),
            pl.BlockSpec((1, D_INNER, bt), lambda i: (i, 0, 0)),
            pl.BlockSpec((bt, 512), lambda i: (i, D_INNER // 512)),
            pl.BlockSpec((bt, 512), lambda i: (i, D_INNER // 512 + 1))],
        out_specs=[
            pl.BlockSpec((bt, D_INNER, D_STATE), lambda i: (i, 0, 0)),
            pl.BlockSpec((1, D_INNER, bt), lambda i: (i, 0, 0))],
        out_shape=[
            jax.ShapeDtypeStruct((m, D_INNER, D_STATE), F32),
            jax.ShapeDtypeStruct((nb, D_INNER, bt), F32)],
        compiler_params=_cparams(1),
    )(h0, cols(xdt), cols(dec), xc, xc)
    return h_new, y_cols.transpose(0, 2, 1).reshape(m, D_INNER)


def _step_post_body(y_ref, xc_ref, z_ref, dskip_ref, ng_ref, yn_ref):
    gw = D_INNER // SSM_GROUPS
    for g in range(SSM_GROUPS):
        gs = slice(g * gw, (g + 1) * gw)
        y = y_ref[:, gs] + dskip_ref[:, gs] * xc_ref[:, gs]
        y = y * _silu(z_ref[:, gs])
        ms = jnp.mean(y * y, axis=-1, keepdims=True)
        yn_ref[:, gs] = (y * lax.rsqrt(ms + NORM_EPS) * ng_ref[:, gs]).astype(BF16)


def _step_post(y, xc, proj, dskip_x, norm_g):
    m = y.shape[0]
    return pl.pallas_call(
        _step_post_body,
        grid=(1,),
        in_specs=[
            pl.BlockSpec((m, D_INNER), lambda i: (0, 0)),
            pl.BlockSpec((m, D_INNER), lambda i: (0, 0)),
            pl.BlockSpec((m, D_INNER), lambda i: (0, COL_Z // D_INNER)),
            pl.BlockSpec((1, D_INNER), lambda i: (0, 0)),
            pl.BlockSpec((1, D_INNER), lambda i: (0, 0))],
        out_specs=pl.BlockSpec((m, D_INNER), lambda i: (0, 0)),
        out_shape=jax.ShapeDtypeStruct((m, D_INNER), BF16),
        compiler_params=_cparams(1),
    )(y, xc, proj, dskip_x, norm_g)


def _row_to_col(row):
    n = row.shape[1]
    r = lax.broadcasted_iota(jnp.int32, (n, n), 0)
    c = lax.broadcasted_iota(jnp.int32, (n, n), 1)
    return jnp.sum(jnp.where(r == c, jnp.broadcast_to(row, (n, n)), 0.0), axis=1, keepdims=True)


def _col_to_row(col):
    n = col.shape[0]
    r = lax.broadcasted_iota(jnp.int32, (n, n), 0)
    c = lax.broadcasted_iota(jnp.int32, (n, n), 1)
    return jnp.sum(jnp.where(r == c, jnp.broadcast_to(col, (n, n)), 0.0), axis=0, keepdims=True)


def _step_window_body(q_ref, kn_ref, vn_ref, win_ref, o_ref, l_ref, wout_ref, *, group, bt):
    window, dil = DIL_GROUPS[group]
    dh = ATTN_HEAD_DIM
    wq = ATTN_OUT_WIDTH
    j = lax.broadcasted_iota(jnp.int32, (1, window), 1)
    read = lax.rem(j, dil) == 0
    dist = (window - j).astype(F32)
    last = lax.broadcasted_iota(jnp.int32, (wq, window), 1) == window - 1
    for jb in range(bt):
        q = q_ref[jb] * (ATTN_HEAD_DIM ** -0.5)
        k_new = kn_ref[jb]
        v_new = vn_ref[jb]
        k_t = win_ref[jb, 0]
        v_t = win_ref[jb, 1]
        qk = k_t * _row_to_col(q)
        ps, extras = [], []
        for h in range(HEADS_PER_DIL_GROUP):
            hs = slice(h * dh, (h + 1) * dh)
            s = jnp.sum(qk[hs, :], axis=0, keepdims=True) - _slope(group * HEADS_PER_DIL_GROUP + h) * dist
            s = jnp.where(read, s, -jnp.inf)
            s_new = jnp.sum(q[:, hs] * k_new[:, hs], axis=-1, keepdims=True)
            m = jnp.maximum(jnp.max(s, axis=-1, keepdims=True), s_new)
            p = jnp.exp(s - m)
            p_new = jnp.exp(s_new - m)
            l = jnp.sum(p, axis=-1, keepdims=True) + p_new
            ps.append(jnp.broadcast_to(p, (dh, window)))
            extras.append((p_new, l, m))
        pv = jnp.sum(v_t * jnp.concatenate(ps, axis=0), axis=1, keepdims=True)
        o_row = _col_to_row(pv)
        for h, (p_new, l, m) in enumerate(extras):
            hs = slice(h * dh, (h + 1) * dh)
            o_ref[jb, :, hs] = (o_row[:, hs] + p_new * v_new[:, hs]) / l
            l_ref[jb, :, hs] = jnp.broadcast_to(m + jnp.log(l), (1, dh))
        wout_ref[jb, 0] = jnp.where(last, _row_to_col(k_new), pltpu.roll(k_t, window - 1, 1))
        wout_ref[jb, 1] = jnp.where(last, _row_to_col(v_new), pltpu.roll(v_t, window - 1, 1))


def _step_window(q, k_new, v_new, win_t, group):
    m = q.shape[0]
    wq = ATTN_OUT_WIDTH
    window = win_t.shape[-1]
    bt = max(1, min(_STEP_BT, (4 * 1024 * 1024) // (2 * wq * window * 4)))
    assert m % bt == 0
    r3 = lambda a: a.reshape(m, 1, wq)
    vec = pl.BlockSpec((bt, 1, wq), lambda i: (i, 0, 0))
    win = pl.BlockSpec((bt, 2, wq, window), lambda i: (i, 0, 0, 0))
    o, lse, win_new = pl.pallas_call(
        functools.partial(_step_window_body, group=group, bt=bt),
        grid=(m // bt,),
        in_specs=[vec, vec, vec, win],
        out_specs=[vec, vec, win],
        out_shape=[jax.ShapeDtypeStruct((m, 1, wq), F32), jax.ShapeDtypeStruct((m, 1, wq), F32),
                   jax.ShapeDtypeStruct(win_t.shape, F32)],
        compiler_params=_cparams(1),
    )(r3(q), r3(k_new), r3(v_new), win_t)
    return o.reshape(m, wq), lse.reshape(m, wq), win_new


def _prep_weights(w_in, conv_b, dt_bias, a_log, d_skip, ssm_norm_g, w_out_ssm, w_out_attn, w_out,
                  ln1_g, ln1_b, w_router, b_router, ln2_g, ln2_b):
    cuts = np.cumsum((D_INNER, CONV_DIM, SSM_HEADS, ATTN_WIDTH, ATTN_WIDTH, ATTN_WIDTH, D_MODEL, D_MODEL))
    z, xbc, dt, q, k, v, ga, gb = jnp.split(w_in, [int(c) for c in cuts[:-1]], axis=1)
    xs, bm, cm = jnp.split(xbc, [D_INNER, D_INNER + 512], axis=1)
    dt_pad = jnp.zeros((D_MODEL, PROJ_W - COL_DT - SSM_HEADS), w_in.dtype)
    w_perm = jnp.concatenate([z, xs, ga, gb, bm, cm, q, k, v, dt, dt_pad], axis=1).astype(BF16)
    pad_heads = lambda a: jnp.pad(a.astype(F32), (0, LANES - SSM_HEADS)).reshape(1, LANES)
    row = lambda a: a.astype(F32).reshape(1, -1)
    return dict(
        w_perm=w_perm, conv_b=row(conv_b), dtb=pad_heads(dt_bias), alog=pad_heads(a_log),
        dskip=row(jnp.repeat(d_skip, SSM_HEAD_DIM)), norm_g=row(ssm_norm_g),
        w_ssm=w_out_ssm.astype(BF16), w_attn=w_out_attn.astype(BF16), w_out=w_out.astype(BF16),
        g1=row(ln1_g), b1=row(ln1_b),
        w_r=jnp.pad(w_router, ((0, 0), (0, LANES - N_EXPERTS))).astype(BF16),
        b_r=jnp.pad(b_router.astype(F32), (0, LANES - N_EXPERTS)).reshape(1, LANES),
        g2=row(ln2_g), b2=row(ln2_b))


def _layer_prompt(x, p, conv_w, moe_w):
    bsz, t, _ = x.shape
    x2d = x.reshape(bsz * t, D_MODEL)
    proj = _in_proj(x2d, p['w_perm'])
    yn, h_new, conv_new = _ssd_prompt(proj, bsz, t, conv_w, p['conv_b'], p['dtb'], p['alog'],
                                      p['dskip'], p['norm_g'])
    attn, kvs = [], []
    for g, (window, _) in enumerate(DIL_GROUPS):
        o, lse, kvt = _attn_prompt(proj, bsz, t, g)
        attn.extend((o, lse))
        kvs.append(_window_from_lanes(kvt))
    x1t, idx_dense, gates_dense, rank_dense, counts = _mix(
        yn, attn, proj, x2d, p['w_ssm'], p['w_attn'], p['w_out'], p['g1'], p['b1'], p['w_r'], p['b_r'])
    y = _moe_and_norm(x1t, idx_dense, gates_dense, rank_dense, counts, *moe_w, p['g2'], p['b2'])
    return y.reshape(bsz, t, D_MODEL), h_new, conv_new, kvs


def _layer_step(x, conv_buf, h0, kv_bufs, p, conv_w, moe_w):
    m = x.shape[0]
    x2d = x.reshape(m, D_MODEL)
    proj = _in_proj(x2d, p['w_perm'])
    xc, conv_new_t, xdt, dec = _step_pre(proj, conv_buf.transpose(1, 0, 2), conv_w, p['conv_b'],
                                         p['dtb'], p['alog'])
    h_new, y = _step_state(h0.reshape(m, D_INNER, D_STATE), xdt, dec, xc)
    yn = _step_post(y, xc, proj, p['dskip'], p['norm_g'])
    wq = ATTN_OUT_WIDTH
    attn, kv_new = [], []
    for g in range(len(DIL_GROUPS)):
        cols = lambda c0: proj[:, c0 + g * wq:c0 + (g + 1) * wq]
        o, lse, win_new = _step_window(cols(COL_Q), cols(COL_K), cols(COL_V),
                                       _window_to_lanes(kv_bufs[g]), g)
        attn.extend((o, lse))
        kv_new.append(_window_from_lanes(win_new))
    x1t, idx_dense, gates_dense, rank_dense, counts = _mix(
        yn, attn, proj, x2d, p['w_ssm'], p['w_attn'], p['w_out'], p['g1'], p['b1'], p['w_r'], p['b_r'])
    y_out = _moe_and_norm(x1t, idx_dense, gates_dense, rank_dense, counts, *moe_w, p['g2'], p['b2'])
    h_new = h_new.reshape(m, SSM_HEADS, SSM_HEAD_DIM, D_STATE)
    return y_out.reshape(m, 1, D_MODEL), h_new, conv_new_t.transpose(1, 0, 2), kv_new


def kernel(x_prompt, x_sample, state_ssm, state_conv, cache_kv_w128, cache_kv_w512, cache_kv_w2048, w_in, conv_w, conv_b, dt_bias, a_log, d_skip, ssm_norm_g, w_out_ssm, w_out_attn, w_out, ln1_g, ln1_b, w_router, b_router, w_gate_up, b_gate_up, w_down, b_down, ln2_g, ln2_b):
    assert w_in.shape[0] == DEPTH == 1 and x_sample.shape[1] == 1
    l = 0
    p = _prep_weights(w_in[l], conv_b[l], dt_bias[l], a_log[l], d_skip[l], ssm_norm_g[l], w_out_ssm[l],
                      w_out_attn[l], w_out[l], ln1_g[l], ln1_b[l], w_router[l], b_router[l], ln2_g[l],
                      ln2_b[l])
    moe_w = (w_gate_up[l], b_gate_up[l], w_down[l], b_down[l])
    y_p, hp, cp, kvp = _layer_prompt(x_prompt, p, conv_w[l], moe_w)
    y_s, hs, cs, kvs = _layer_step(x_sample, state_conv[l], state_ssm[l],
                                   (cache_kv_w128[l], cache_kv_w512[l], cache_kv_w2048[l]), p,
                                   conv_w[l], moe_w)
    stack = lambda a: a[None]
    return (y_p, y_s, stack(hp), stack(cp), stack(kvp[0]), stack(kvp[1]), stack(kvp[2]),
            stack(hs), stack(cs), stack(kvs[0]), stack(kvs[1]), stack(kvs[2]))
```

```python
import functools

import numpy as np
import jax
import jax.numpy as jnp
from jax import lax
from jax.experimental import pallas as pl
from jax.experimental.pallas import tpu as pltpu

F32 = jnp.float32
BF16 = jnp.bfloat16

D_MODEL = 1024
D_INNER = 2048
SSM_HEAD_DIM = 64
SSM_HEADS = 32
SSM_GROUPS = 4
D_STATE = 128
CONV_WIDTH = 4
CONV_DIM = D_INNER + 2 * SSM_GROUPS * D_STATE
SSD_CHUNK = 128
ATTN_HEAD_DIM = 64
HEADS_PER_DIL_GROUP = 4
DIL_GROUPS = ((128, 1), (512, 4), (2048, 16))
N_ATTN_HEADS = HEADS_PER_DIL_GROUP * len(DIL_GROUPS)
ATTN_WIDTH = N_ATTN_HEADS * ATTN_HEAD_DIM
ATTN_OUT_WIDTH = HEADS_PER_DIL_GROUP * ATTN_HEAD_DIM
ATTN_STEPS = 128
N_EXPERTS = 32
TOP_K = 4
D_FF = D_MODEL
SWIGLU_LIMIT = 7.0
SWIGLU_ALPHA = 1.702
DEPTH = 1
ALPHA = (2.0 * DEPTH) ** 0.25
NORM_EPS = 1e-5

LANES = 128
SUBLANES = 8
VMEM_LIMIT = 56 * 1024 * 1024

COL_Z = 0
COL_XS = 2048
COL_GA = 4096
COL_GB = 5120
COL_B = 6144
COL_C = 6656
COL_Q = 7168
COL_K = 7936
COL_V = 8704
COL_DT = 9472
PROJ_W = 9728

MOE_ROWS = 512


def _cparams(n_grid):
    return pltpu.CompilerParams(dimension_semantics=("arbitrary",) * n_grid,
                                vmem_limit_bytes=VMEM_LIMIT)


def _sigmoid(x):
    return 1.0 / (1.0 + jnp.exp(-x))


def _silu(x):
    return x * _sigmoid(x)


def _softplus(x):
    return jnp.maximum(x, 0.0) + jnp.log1p(jnp.exp(-jnp.abs(x)))


def _layer_norm(v, g, b):
    mu = jnp.mean(v, axis=-1, keepdims=True)
    d = v - mu
    var = jnp.mean(d * d, axis=-1, keepdims=True)
    return d * lax.rsqrt(var + NORM_EPS) * g + b


def _dot_nt(a, b):
    return lax.dot_general(a, b, (((1,), (1,)), ((), ())), preferred_element_type=F32)


def _dot_tn(a, b):
    return lax.dot_general(a, b, (((0,), (0,)), ((), ())), preferred_element_type=F32)


def _dot_exact(a, b):
    return jnp.dot(a, b, preferred_element_type=F32, precision=lax.Precision.HIGHEST)


ROW_TILE = D_MODEL // LANES
assert ROW_TILE == SUBLANES


def _rows_to_tiles(x, tiles_ref, base=0):
    n = x.shape[0]
    for c in range(ROW_TILE):
        tiles_ref[pl.ds(base + c, n, stride=ROW_TILE), :] = x[:, c * LANES:(c + 1) * LANES]


def _tile_column(tiles_ref, n, c, base=0):
    return tiles_ref[pl.ds(base + c, n, stride=ROW_TILE), :]


def _proj_body(x_ref, w_ref, o_ref, xb_ref):
    @pl.when(pl.program_id(1) == 0)
    def _():
        xb_ref[...] = x_ref[...].astype(BF16)

    o_ref[...] = jnp.dot(xb_ref[...], w_ref[...], preferred_element_type=F32)


def _in_proj(x2d, w_perm):
    m = x2d.shape[0]
    tm = min(2048, m)
    tn = 512
    assert m % tm == 0 and PROJ_W % tn == 0
    return pl.pallas_call(
        _proj_body,
        grid=(m // tm, PROJ_W // tn),
        in_specs=[pl.BlockSpec((tm, D_MODEL), lambda i, j: (i, 0)),
                  pl.BlockSpec((D_MODEL, tn), lambda i, j: (0, j))],
        out_specs=pl.BlockSpec((tm, tn), lambda i, j: (i, j)),
        out_shape=jax.ShapeDtypeStruct((m, PROJ_W), F32),
        scratch_shapes=[pltpu.VMEM((tm, D_MODEL), BF16)],
        compiler_params=_cparams(2),
    )(x2d, w_perm)


_PAIR = 2 * SSM_HEAD_DIM
_N_PAIRS = SSM_HEADS // 2
_PAIRS_PER_GROUP = _N_PAIRS // SSM_GROUPS
_EXT_PAD = SUBLANES


def _ssd_body(z_ref, xs_ref, b_ref, c_ref, dt_ref, cw_ref, cb_ref, dtb_ref, alog_ref, dskip_ref,
              ng_ref, yn_ref, hout_ref, convout_ref, ext_ref, xc_ref, s_ref, ybuf_ref):
    c = pl.program_id(1)
    nc = pl.num_programs(1)
    lc = SSD_CHUNK

    @pl.when(c == 0)
    def _():
        ext_ref[0:_EXT_PAD, :] = jnp.zeros((_EXT_PAD, CONV_DIM), F32)
        s_ref[...] = jnp.zeros(s_ref.shape, F32)

    ext_ref[_EXT_PAD:_EXT_PAD + lc, 0:D_INNER] = xs_ref[...]
    ext_ref[_EXT_PAD:_EXT_PAD + lc, D_INNER:D_INNER + 512] = b_ref[...]
    ext_ref[_EXT_PAD:_EXT_PAD + lc, D_INNER + 512:CONV_DIM] = c_ref[...]

    cstep = 256
    for c0 in range(0, CONV_DIM, cstep):
        cs = slice(c0, c0 + cstep)
        acc = cb_ref[:, cs] + cw_ref[CONV_WIDTH - 1:CONV_WIDTH, cs] * ext_ref[_EXT_PAD:_EXT_PAD + lc, cs]
        for s in range(1, CONV_WIDTH):
            acc = acc + (cw_ref[CONV_WIDTH - 1 - s:CONV_WIDTH - s, cs]
                         * ext_ref[_EXT_PAD - s:_EXT_PAD - s + lc, cs])
        xc_ref[:, cs] = _silu(acc)

    @pl.when(c == nc - 1)
    def _():
        convout_ref[0] = ext_ref[_EXT_PAD + lc - (CONV_WIDTH - 1):_EXT_PAD + lc, :]

    ext_ref[0:_EXT_PAD, :] = ext_ref[lc:lc + _EXT_PAD, :]

    dtv = _softplus(dt_ref[...] + dtb_ref[...])
    a_neg = -jnp.exp(alog_ref[...])
    d_a = dtv * a_neg
    row = lax.broadcasted_iota(jnp.int32, (lc, lc), 0)
    col = lax.broadcasted_iota(jnp.int32, (lc, lc), 1)
    causal = row >= col
    tril = causal.astype(F32)
    a_cum = _dot_exact(tril, d_a)
    a_cum_t = a_cum.T
    a_last = a_cum[lc - 1:lc, :]
    chunk_decay = jnp.exp(a_last)
    decay_end = jnp.exp(a_last - a_cum)
    exp_a = jnp.exp(a_cum)
    dte = dtv * decay_end

    lane = lax.broadcasted_iota(jnp.int32, (lc, LANES), 1)
    first = lane < SSM_HEAD_DIM
    lane_row = lax.broadcasted_iota(jnp.int32, (1, LANES), 1)
    first_row = lane_row < SSM_HEAD_DIM

    def pick(arr, h0):
        return jnp.where(first, arr[:, h0:h0 + 1], arr[:, h0 + 1:h0 + 2])

    for g in range(SSM_GROUPS):
        bg = xc_ref[:, D_INNER + g * D_STATE:D_INNER + (g + 1) * D_STATE]
        cg = xc_ref[:, D_INNER + 512 + g * D_STATE:D_INNER + 512 + (g + 1) * D_STATE]
        bg_b = bg.astype(BF16)
        cb = _dot_nt(cg.astype(BF16), bg_b)
        for j in range(_PAIRS_PER_GROUP):
            pi = g * _PAIRS_PER_GROUP + j
            h0 = 2 * pi
            xs_pair = xc_ref[:, pi * _PAIR:(pi + 1) * _PAIR]
            xdt = xs_pair * pick(dtv, h0)
            xdte = xs_pair * pick(dte, h0)
            lhs = []
            for h in (h0, h0 + 1):
                seg = a_cum[:, h:h + 1] - a_cum_t[h:h + 1, :]
                lhs.append((cb * jnp.exp(jnp.where(causal, seg, -jnp.inf))).astype(BF16))
            for h in (h0, h0 + 1):
                lhs.append((cg * exp_a[:, h:h + 1]).astype(BF16))
            lhs = jnp.concatenate(lhs, axis=1)
            s_old = s_ref[pi]
            zero = jnp.zeros_like(xdt)
            rhs = jnp.concatenate([jnp.where(first, xdt, zero), jnp.where(first, zero, xdt),
                                   jnp.where(first, s_old, zero), jnp.where(first, zero, s_old)],
                                  axis=0).astype(BF16)
            ybuf_ref[:, pi * _PAIR:(pi + 1) * _PAIR] = jnp.dot(lhs, rhs, preferred_element_type=F32)
            cd = jnp.where(first_row, chunk_decay[:, h0:h0 + 1], chunk_decay[:, h0 + 1:h0 + 2])
            s_ref[pi] = cd * s_old + _dot_tn(bg_b, xdte.astype(BF16))

    gw = D_INNER // SSM_GROUPS
    for g in range(SSM_GROUPS):
        gs = slice(g * gw, (g + 1) * gw)
        y = ybuf_ref[:, gs] + dskip_ref[:, gs] * xc_ref[:, gs]
        y = y * _silu(z_ref[:, gs])
        ms = jnp.mean(y * y, axis=-1, keepdims=True)
        yn_ref[:, gs] = (y * lax.rsqrt(ms + NORM_EPS) * ng_ref[:, gs]).astype(BF16)

    @pl.when(c == nc - 1)
    def _():
        for pi in range(_N_PAIRS):
            t = s_ref[pi].T
            hout_ref[0, 2 * pi] = t[0:SSM_HEAD_DIM]
            hout_ref[0, 2 * pi + 1] = t[SSM_HEAD_DIM:_PAIR]


def _ssd_prompt(proj, bsz, t, conv_w, conv_b, dtb_pad, alog_pad, dskip_x, norm_g):
    lc = SSD_CHUNK
    assert t % lc == 0
    nc = t // lc
    row = lambda b, c: b * nc + c
    const = lambda b, c: (0, 0)
    return pl.pallas_call(
        _ssd_body,
        grid=(bsz, nc),
        in_specs=[
            pl.BlockSpec((lc, D_INNER), lambda b, c: (row(b, c), COL_Z // D_INNER)),
            pl.BlockSpec((lc, D_INNER), lambda b, c: (row(b, c), COL_XS // D_INNER)),
            pl.BlockSpec((lc, 512), lambda b, c: (row(b, c), COL_B // 512)),
            pl.BlockSpec((lc, 512), lambda b, c: (row(b, c), COL_C // 512)),
            pl.BlockSpec((lc, LANES), lambda b, c: (row(b, c), COL_DT // LANES)),
            pl.BlockSpec((CONV_WIDTH, CONV_DIM), const),
            pl.BlockSpec((1, CONV_DIM), const),
            pl.BlockSpec((1, LANES), const),
            pl.BlockSpec((1, LANES), const),
            pl.BlockSpec((1, D_INNER), const),
            pl.BlockSpec((1, D_INNER), const),
        ],
        out_specs=[
            pl.BlockSpec((lc, D_INNER), lambda b, c: (row(b, c), 0)),
            pl.BlockSpec((1, SSM_HEADS, SSM_HEAD_DIM, D_STATE), lambda b, c: (b, 0, 0, 0)),
            pl.BlockSpec((1, CONV_WIDTH - 1, CONV_DIM), lambda b, c: (b, 0, 0)),
        ],
        out_shape=[
            jax.ShapeDtypeStruct((bsz * t, D_INNER), BF16),
            jax.ShapeDtypeStruct((bsz, SSM_HEADS, SSM_HEAD_DIM, D_STATE), F32),
            jax.ShapeDtypeStruct((bsz, CONV_WIDTH - 1, CONV_DIM), F32),
        ],
        scratch_shapes=[
            pltpu.VMEM((lc + _EXT_PAD, CONV_DIM), F32),
            pltpu.VMEM((lc, CONV_DIM), F32),
            pltpu.VMEM((_N_PAIRS, D_STATE, _PAIR), F32),
            pltpu.VMEM((lc, D_INNER), F32),
        ],
        compiler_params=_cparams(2),
    )(proj, proj, proj, proj, proj, conv_w, conv_b, dtb_pad, alog_pad, dskip_x, norm_g)


def _slope(head):
    return float(np.float32(2.0 ** (-8.0 * (head + 1) / N_ATTN_HEADS)))


_HEADS_PER_SLAB = LANES // ATTN_HEAD_DIM
_SLABS_PER_GROUP = HEADS_PER_DIL_GROUP // _HEADS_PER_SLAB


def _attn_body(q_ref, k_ref, v_ref, o_ref, lse_ref, kvt_ref, *, group, dil):
    slab = pl.program_id(1)
    st = ATTN_STEPS
    dh = ATTN_HEAD_DIM
    t = q_ref.shape[0]
    rb = st * dil
    kvt_ref[0, 0] = k_ref[t - rb:t, :].T
    kvt_ref[0, 1] = v_ref[t - rb:t, :].T

    def mask_and_dist(n_keys):
        i = lax.broadcasted_iota(jnp.int32, (st, n_keys), 0)
        j = lax.broadcasted_iota(jnp.int32, (st, n_keys), 1)
        d_sub = i + (n_keys - st) - j
        return (d_sub >= 0) & (d_sub <= st), (d_sub * dil).astype(F32)

    first = mask_and_dist(st)
    later = mask_and_dist(2 * st)

    def rows(start, n):
        return pl.ds(start, n, stride=dil) if dil > 1 else pl.ds(start, n)

    for blk in range(t // rb):
        valid, dist = first if blk == 0 else later
        n_keys = st if blk == 0 else 2 * st
        for r in range(dil):
            q_rows = rows(blk * rb + r, st)
            k_rows = rows(max(blk - 1, 0) * rb + r, n_keys)
            q = q_ref[q_rows, :] * (ATTN_HEAD_DIM ** -0.5)
            k2 = k_ref[k_rows, :].astype(BF16)
            v2 = v_ref[k_rows, :].astype(BF16)
            o_parts, lse_parts = [], []
            for h in range(_HEADS_PER_SLAB):
                hs = slice(h * dh, (h + 1) * dh)
                base = group * HEADS_PER_DIL_GROUP + h
                slope = jnp.where(slab == 0, _slope(base), _slope(base + _HEADS_PER_SLAB))
                s = _dot_nt(q[:, hs].astype(BF16), k2[:, hs]) - slope * dist
                s = jnp.where(valid, s, -jnp.inf)
                m = jnp.max(s, axis=-1, keepdims=True)
                p = jnp.exp(s - m)
                l = jnp.sum(p, axis=-1, keepdims=True)
                o_parts.append(jnp.dot(p.astype(BF16), v2[:, hs], preferred_element_type=F32) / l)
                lse_parts.append(jnp.broadcast_to(m + jnp.log(l), (st, dh)))
            o_ref[q_rows, :] = jnp.concatenate(o_parts, axis=1)
            lse_ref[q_rows, :] = jnp.concatenate(lse_parts, axis=1)


def _attn_prompt(proj, bsz, t, group):
    window, dil = DIL_GROUPS[group]
    rb = ATTN_STEPS * dil
    assert window == rb and t % rb == 0 and _SLABS_PER_GROUP == 2
    wq = ATTN_OUT_WIDTH
    col = lambda c0: c0 // LANES + group * _SLABS_PER_GROUP
    seq = lambda c0: pl.BlockSpec((t, LANES), lambda b, s: (b, col(c0) + s))
    return pl.pallas_call(
        functools.partial(_attn_body, group=group, dil=dil),
        grid=(bsz, _SLABS_PER_GROUP),
        in_specs=[seq(COL_Q), seq(COL_K), seq(COL_V)],
        out_specs=[pl.BlockSpec((t, LANES), lambda b, s: (b, s)),
                   pl.BlockSpec((t, LANES), lambda b, s: (b, s)),
                   pl.BlockSpec((1, 2, LANES, window), lambda b, s: (b, 0, s, 0))],
        out_shape=[jax.ShapeDtypeStruct((bsz * t, wq), F32),
                   jax.ShapeDtypeStruct((bsz * t, wq), F32),
                   jax.ShapeDtypeStruct((bsz, 2, wq, window), F32)],
        compiler_params=_cparams(2),
    )(proj, proj, proj)


def _window_from_lanes(kvt):
    bsz, _, _, window = kvt.shape
    return kvt.reshape(bsz, 2, HEADS_PER_DIL_GROUP, ATTN_HEAD_DIM, window).transpose(0, 4, 1, 2, 3)


def _window_to_lanes(buf):
    bsz, window = buf.shape[:2]
    return buf.transpose(0, 2, 3, 4, 1).reshape(bsz, 2, ATTN_OUT_WIDTH, window)


def _mix_body(yn_ref, o0_ref, l0_ref, o1_ref, l1_ref, o2_ref, l2_ref, ga_ref, gb_ref, x_ref,
              wssm_ref, wattn_ref, wout_ref, g1_ref, b1_ref, wr_ref, br_ref,
              x1t_ref, idx_ref, gate_ref, rank_ref, cnt_ref, carry_ref):
    @pl.when(pl.program_id(0) == 0)
    def _():
        carry_ref[...] = jnp.zeros(carry_ref.shape, F32)

    branch_a = jnp.dot(yn_ref[...], wssm_ref[...], preferred_element_type=F32)
    l0, l1, l2 = l0_ref[...], l1_ref[...], l2_ref[...]
    m = jnp.maximum(jnp.maximum(l0, l1), l2)
    e0, e1, e2 = jnp.exp(l0 - m), jnp.exp(l1 - m), jnp.exp(l2 - m)
    o = (e0 * o0_ref[...] + e1 * o1_ref[...] + e2 * o2_ref[...]) / (e0 + e1 + e2)
    branch_b = jnp.dot(o.astype(BF16), wattn_ref[...], preferred_element_type=F32)
    merged = _sigmoid(ga_ref[...]) * branch_a + _sigmoid(gb_ref[...]) * branch_b
    mix = jnp.dot(merged.astype(BF16), wout_ref[...], preferred_element_type=F32)
    x1 = _layer_norm(ALPHA * x_ref[...] + mix, g1_ref[...], b1_ref[...])
    _rows_to_tiles(x1, x1t_ref)

    logits = jnp.dot(x1.astype(BF16), wr_ref[...], preferred_element_type=F32) + br_ref[...]
    lane = lax.broadcasted_iota(jnp.int32, logits.shape, 1)
    logits = jnp.where(lane < N_EXPERTS, logits, -jnp.inf)
    vals, idxs = [], []
    for _ in range(TOP_K):
        mk = jnp.max(logits, axis=-1, keepdims=True)
        ik = jnp.min(jnp.where(logits == mk, lane, LANES), axis=-1, keepdims=True)
        vals.append(mk)
        idxs.append(ik)
        logits = jnp.where(lane == ik, -jnp.inf, logits)
    es = [jnp.exp(v - vals[0]) for v in vals]
    den = es[0] + es[1] + es[2] + es[3]
    tm = lane.shape[0]
    onehot = jnp.zeros(lane.shape, F32)
    for k in range(TOP_K):
        onehot = onehot + (lane == idxs[k]).astype(F32)
    ri = lax.broadcasted_iota(jnp.int32, (tm, tm), 0)
    ci = lax.broadcasted_iota(jnp.int32, (tm, tm), 1)
    before = (ri > ci).astype(BF16)
    prefix = jnp.dot(before, onehot.astype(BF16), preferred_element_type=F32) + carry_ref[...]
    idx_out = jnp.zeros(lane.shape, jnp.int32)
    gate_out = jnp.zeros(lane.shape, F32)
    rank_out = jnp.zeros(lane.shape, jnp.int32)
    for k in range(TOP_K):
        rank_k = jnp.sum(jnp.where(lane == idxs[k], prefix, 0.0), axis=-1, keepdims=True)
        idx_out = jnp.where(lane == k, idxs[k], idx_out)
        gate_out = jnp.where(lane == k, es[k] / den, gate_out)
        rank_out = jnp.where(lane == k, rank_k.astype(jnp.int32), rank_out)
    idx_ref[...] = idx_out
    gate_ref[...] = gate_out
    rank_ref[...] = rank_out
    carry_ref[...] = carry_ref[...] + jnp.sum(onehot, axis=0, keepdims=True)
    cnt_ref[...] = carry_ref[...].astype(jnp.int32)


def _mix(yn, attn, proj, x2d, w_ssm, w_attn, w_out, g1, b1, w_r, b_r):
    m = x2d.shape[0]
    tm = min(256, m)
    assert m % tm == 0
    rowblk = lambda w: pl.BlockSpec((tm, w), lambda i: (i, 0))
    const = lambda a: pl.BlockSpec(a.shape, lambda i: (0,) * a.ndim)
    attn_specs = [rowblk(ATTN_OUT_WIDTH)] * 6
    return pl.pallas_call(
        _mix_body,
        grid=(m // tm,),
        in_specs=[rowblk(D_INNER)] + attn_specs + [
            pl.BlockSpec((tm, D_MODEL), lambda i: (i, COL_GA // D_MODEL)),
            pl.BlockSpec((tm, D_MODEL), lambda i: (i, COL_GB // D_MODEL)),
            rowblk(D_MODEL),
            const(w_ssm), const(w_attn), const(w_out), const(g1), const(b1), const(w_r), const(b_r)],
        out_specs=[pl.BlockSpec((tm * ROW_TILE, LANES), lambda i: (i, 0)),
                   rowblk(LANES), rowblk(LANES), rowblk(LANES),
                   pl.BlockSpec((1, LANES), lambda i: (0, 0))],
        out_shape=[jax.ShapeDtypeStruct((m * ROW_TILE, LANES), F32),
                   jax.ShapeDtypeStruct((m, LANES), jnp.int32),
                   jax.ShapeDtypeStruct((m, LANES), F32),
                   jax.ShapeDtypeStruct((m, LANES), jnp.int32),
                   jax.ShapeDtypeStruct((1, LANES), jnp.int32)],
        scratch_shapes=[pltpu.VMEM((1, LANES), F32)],
        compiler_params=_cparams(1),
    )(yn, *attn, proj, proj, x2d, w_ssm, w_attn, w_out, g1, b1, w_r, b_r)


def _tile_copy(idx_ref, r, src_hbm, dst, sem):
    t = pl.multiple_of(idx_ref[0, 0, r] * ROW_TILE, ROW_TILE)
    d = r * ROW_TILE if isinstance(r, int) else pl.multiple_of(r * ROW_TILE, ROW_TILE)
    return pltpu.make_async_copy(src_hbm.at[pl.ds(t, ROW_TILE)], dst.at[pl.ds(d, ROW_TILE)], sem)


def _row_gather(idx_ref, n_rows, src_hbm, dst, sem):
    def body(r, carry):
        _tile_copy(idx_ref, r, src_hbm, dst, sem).start()
        return carry
    lax.fori_loop(0, n_rows, body, 0, unroll=8)


def _row_gather_unrolled(idx_ref, n_rows, src_hbm, dst, sem):
    for r in range(n_rows):
        _tile_copy(idx_ref, r, src_hbm, dst, sem).start()


def _row_gather_wait(n_rows, src_hbm, dst, sem):
    pltpu.make_async_copy(src_hbm.at[pl.ds(0, n_rows * ROW_TILE)], dst, sem).wait()


def _expert_body(blk_exp_ref, n_used_ref, tok_cur_ref, tok_nxt_ref, x_hbm, wgu_ref, bgu_ref, wd_ref,
                 bd_ref, out_ref, xbuf, sem, wgu_b, wd_b):
    i = pl.program_id(0)
    n_used = n_used_ref[0]
    tb = MOE_ROWS
    parity = lax.rem(i, 2)

    @pl.when(i == 0)
    def _():
        _row_gather(tok_cur_ref, tb, x_hbm, xbuf.at[0], sem.at[0])

    e = blk_exp_ref[i]
    e_prev = blk_exp_ref[jnp.maximum(i - 1, 0)]

    @pl.when((i == 0) | ((e != e_prev) & (i < n_used)))
    def _():
        wgu_b[...] = wgu_ref[0].astype(BF16)
        wd_b[...] = wd_ref[0].astype(BF16)

    def block(slot):
        _row_gather_unrolled(tok_nxt_ref, tb, x_hbm, xbuf.at[1 - slot], sem.at[1 - slot])
        _row_gather_wait(tb, x_hbm, xbuf.at[slot], sem.at[slot])
        x = jnp.concatenate([_tile_column(xbuf.at[slot], tb, c).astype(BF16) for c in range(ROW_TILE)],
                            axis=1)
        h = jnp.dot(x, wgu_b[...], preferred_element_type=F32) + bgu_ref[0]
        gate = jnp.minimum(h[:, :D_FF], SWIGLU_LIMIT)
        up = jnp.clip(h[:, D_FF:], -SWIGLU_LIMIT, SWIGLU_LIMIT)
        hmid = (up + 1.0) * (gate * _sigmoid(SWIGLU_ALPHA * gate))
        _rows_to_tiles(jnp.dot(hmid.astype(BF16), wd_b[...], preferred_element_type=F32) + bd_ref[0],
                       out_ref)

    for slot in (0, 1):
        pl.when((i < n_used) & (parity == slot))(functools.partial(block, slot))

    @pl.when(i == n_used - 1)
    def _():
        _row_gather_wait(tb, x_hbm, xbuf.at[1 - parity], sem.at[1 - parity])

    @pl.when(i >= n_used)
    def _():
        out_ref[...] = jnp.zeros(out_ref.shape, F32)


def _experts(x1t, blk_exp, n_used, slot_tok, w_gate_up, b_gate_up, w_down, b_down):
    tb = MOE_ROWS
    n_blk = blk_exp.shape[0]
    tok3 = slot_tok.reshape(n_blk, 1, tb)
    grid_spec = pltpu.PrefetchScalarGridSpec(
        num_scalar_prefetch=2,
        grid=(n_blk,),
        in_specs=[
            pl.BlockSpec((1, 1, tb), lambda i, be, nu: (i, 0, 0), memory_space=pltpu.SMEM),
            pl.BlockSpec((1, 1, tb), lambda i, be, nu: (jnp.minimum(i + 1, n_blk - 1), 0, 0),
                         memory_space=pltpu.SMEM),
            pl.BlockSpec(memory_space=pl.ANY),
            pl.BlockSpec((1, D_MODEL, 2 * D_FF), lambda i, be, nu: (be[i], 0, 0)),
            pl.BlockSpec((1, 1, 2 * D_FF), lambda i, be, nu: (be[i], 0, 0)),
            pl.BlockSpec((1, D_FF, D_MODEL), lambda i, be, nu: (be[i], 0, 0)),
            pl.BlockSpec((1, 1, D_MODEL), lambda i, be, nu: (be[i], 0, 0)),
        ],
        out_specs=pl.BlockSpec((tb * ROW_TILE, LANES), lambda i, be, nu: (i, 0)),
        scratch_shapes=[
            pltpu.VMEM((2, tb * ROW_TILE, LANES), F32),
            pltpu.SemaphoreType.DMA((2,)),
            pltpu.VMEM((D_MODEL, 2 * D_FF), BF16),
            pltpu.VMEM((D_FF, D_MODEL), BF16),
        ],
    )
    return pl.pallas_call(
        _expert_body,
        grid_spec=grid_spec,
        out_shape=jax.ShapeDtypeStruct((n_blk * tb * ROW_TILE, LANES), F32),
        compiler_params=_cparams(1),
    )(blk_exp, n_used, tok3, tok3, x1t, w_gate_up, b_gate_up.reshape(N_EXPERTS, 1, 2 * D_FF),
      w_down, b_down.reshape(N_EXPERTS, 1, D_MODEL))


_COMBINE_TOKENS = 128


def _combine_body(dst_cur_ref, dst_nxt_ref, rows_hbm, gate_ref, x1t_ref, g2_ref, b2_ref, y_ref,
                  buf, sem, acc_ref):
    i = pl.program_id(0)
    n = pl.num_programs(0)
    tt = _COMBINE_TOKENS
    nr = TOP_K * tt
    parity = lax.rem(i, 2)

    @pl.when(i == 0)
    def _():
        _row_gather(dst_cur_ref, nr, rows_hbm, buf.at[0], sem.at[0])

    def tile(slot):
        _row_gather_unrolled(dst_nxt_ref, nr, rows_hbm, buf.at[1 - slot], sem.at[1 - slot])
        _row_gather_wait(nr, rows_hbm, buf.at[slot], sem.at[slot])
        gates = gate_ref[...]
        gk = [jnp.broadcast_to(gates[:, k:k + 1], (tt, LANES)) for k in range(TOP_K)]
        for c in range(ROW_TILE):
            acc = ALPHA * _tile_column(x1t_ref, tt, c)
            for k in range(TOP_K):
                acc = acc + gk[k] * _tile_column(buf.at[slot], tt, c, base=k * tt * ROW_TILE)
            acc_ref[:, c * LANES:(c + 1) * LANES] = acc
        y_ref[...] = _layer_norm(acc_ref[...], g2_ref[...], b2_ref[...])

    for slot in (0, 1):
        pl.when(parity == slot)(functools.partial(tile, slot))

    @pl.when(i == n - 1)
    def _():
        _row_gather_wait(nr, rows_hbm, buf.at[1 - parity], sem.at[1 - parity])


def _combine(rows, dest, gates_dense, x1t, g2, b2):
    m = x1t.shape[0] // ROW_TILE
    tt = _COMBINE_TOKENS
    assert m % tt == 0
    nt = m // tt
    dst3 = dest.reshape(nt, tt, TOP_K).transpose(0, 2, 1).reshape(nt, 1, TOP_K * tt)
    const = lambda a: pl.BlockSpec(a.shape, lambda i: (0,) * a.ndim)
    return pl.pallas_call(
        _combine_body,
        grid=(nt,),
        in_specs=[
            pl.BlockSpec((1, 1, TOP_K * tt), lambda i: (i, 0, 0), memory_space=pltpu.SMEM),
            pl.BlockSpec((1, 1, TOP_K * tt), lambda i: (jnp.minimum(i + 1, nt - 1), 0, 0),
                         memory_space=pltpu.SMEM),
            pl.BlockSpec(memory_space=pl.ANY),
            pl.BlockSpec((tt, LANES), lambda i: (i, 0)),
            pl.BlockSpec((tt * ROW_TILE, LANES), lambda i: (i, 0)),
            const(g2), const(b2),
        ],
        out_specs=pl.BlockSpec((tt, D_MODEL), lambda i: (i, 0)),
        out_shape=jax.ShapeDtypeStruct((m, D_MODEL), F32),
        scratch_shapes=[pltpu.VMEM((2, TOP_K * tt * ROW_TILE, LANES), F32), pltpu.SemaphoreType.DMA((2,)),
                        pltpu.VMEM((tt, D_MODEL), F32)],
        compiler_params=_cparams(1),
    )(dst3, dst3, rows, gates_dense, x1t, g2, b2)


def _route(top_idx, rank, counts):
    tb = MOE_ROWS
    n_tok = top_idx.shape[0]
    n_asg = n_tok * TOP_K
    start = jnp.cumsum(counts) - counts
    padded = (counts + tb - 1) // tb * tb
    pend = jnp.cumsum(padded)
    pstart = pend - padded
    n_blk = -(-n_asg // tb) + N_EXPERTS
    blk_lo = jnp.arange(n_blk, dtype=jnp.int32) * tb
    blk_exp = jnp.minimum(jnp.sum(blk_lo[:, None] >= pend[None, :], axis=1), N_EXPERTS - 1).astype(jnp.int32)
    n_used = (pend[N_EXPERTS - 1] // tb).astype(jnp.int32).reshape(1)
    order = jnp.argsort(top_idx.reshape(-1)).astype(jnp.int32)
    slot = jnp.arange(n_blk * tb, dtype=jnp.int32)
    slot_e = jnp.repeat(blk_exp, tb)
    within = slot - pstart[slot_e]
    src = jnp.clip(start[slot_e] + within, 0, n_asg - 1)
    slot_tok = jnp.where(within < counts[slot_e], order[src] // TOP_K, 0).astype(jnp.int32)
    dest = (pstart[top_idx] + rank).astype(jnp.int32)
    return blk_exp, n_used, slot_tok, dest


def _moe_and_norm(x1t, idx_dense, gates_dense, rank_dense, counts, w_gate_up, b_gate_up, w_down,
                  b_down, g2, b2):
    blk_exp, n_used, slot_tok, dest = _route(idx_dense[:, :TOP_K], rank_dense[:, :TOP_K],
                                             counts[0, :N_EXPERTS])
    rows = _experts(x1t, blk_exp, n_used, slot_tok, w_gate_up, b_gate_up, w_down, b_down)
    return _combine(rows, dest, gates_dense, x1t, g2, b2)


def _expand_heads(v):
    r = lax.broadcasted_iota(jnp.int32, (LANES, D_INNER), 0)
    c = lax.broadcasted_iota(jnp.int32, (LANES, D_INNER), 1)
    sel = (c // SSM_HEAD_DIM == r).astype(F32)
    return _dot_exact(v, sel)


def _step_pre_body(xs_ref, b_ref, c_ref, dt_ref, conv_ref, cw_ref, cb_ref, dtb_ref, alog_ref,
                   xc_ref, convout_ref, xdt_ref, dec_ref):
    w = CONV_WIDTH
    for (c0, c1, src) in ((0, D_INNER, xs_ref), (D_INNER, D_INNER + 512, b_ref),
                          (D_INNER + 512, CONV_DIM, c_ref)):
        cs = slice(c0, c1)
        new = src[...]
        acc = cb_ref[:, cs] + cw_ref[w - 1:w, cs] * new
        for s in range(w - 1):
            acc = acc + cw_ref[s:s + 1, cs] * conv_ref[s, :, cs]
        xc_ref[:, cs] = _silu(acc)
        for s in range(w - 2):
            convout_ref[s, :, cs] = conv_ref[s + 1, :, cs]
        convout_ref[w - 2, :, cs] = new
    dtv = _softplus(dt_ref[...] + dtb_ref[...])
    dec = jnp.exp(dtv * (-jnp.exp(alog_ref[...])))
    xdt_ref[...] = xc_ref[:, 0:D_INNER] * _expand_heads(dtv)
    dec_ref[...] = _expand_heads(dec)


def _step_pre(proj, conv_t, conv_w, conv_b, dtb_pad, alog_pad):
    m = proj.shape[0]
    const2 = lambda a: pl.BlockSpec(a.shape, lambda i: (0,) * a.ndim)
    return pl.pallas_call(
        _step_pre_body,
        grid=(1,),
        in_specs=[
            pl.BlockSpec((m, D_INNER), lambda i: (0, COL_XS // D_INNER)),
            pl.BlockSpec((m, 512), lambda i: (0, COL_B // 512)),
            pl.BlockSpec((m, 512), lambda i: (0, COL_C // 512)),
            pl.BlockSpec((m, LANES), lambda i: (0, COL_DT // LANES)),
            const2(conv_t), const2(conv_w), const2(conv_b), const2(dtb_pad), const2(alog_pad)],
        out_specs=[
            pl.BlockSpec((m, CONV_DIM), lambda i: (0, 0)),
            pl.BlockSpec((CONV_WIDTH - 1, m, CONV_DIM), lambda i: (0, 0, 0)),
            pl.BlockSpec((m, D_INNER), lambda i: (0, 0)),
            pl.BlockSpec((m, D_INNER), lambda i: (0, 0))],
        out_shape=[
            jax.ShapeDtypeStruct((m, CONV_DIM), F32),
            jax.ShapeDtypeStruct((CONV_WIDTH - 1, m, CONV_DIM), F32),
            jax.ShapeDtypeStruct((m, D_INNER), F32),
            jax.ShapeDtypeStruct((m, D_INNER), F32)],
        compiler_params=_cparams(1),
    )(proj, proj, proj, proj, conv_t, conv_w, conv_b, dtb_pad, alog_pad)


_STEP_BT = 8


def _step_state_body(h_ref, xdt_ref, dec_ref, b_ref, c_ref, hout_ref, y_ref):
    rows = 128
    per_group = D_INNER // SSM_GROUPS
    lane = lax.broadcasted_iota(jnp.int32, (rows, _STEP_BT), 1)
    for r0 in range(0, D_INNER, rows):
        g = r0 // per_group
        ycols = jnp.zeros((rows, _STEP_BT), F32)
        for jb in range(_STEP_BT):
            brow = b_ref[jb:jb + 1, g * D_STATE:(g + 1) * D_STATE]
            crow = c_ref[jb:jb + 1, g * D_STATE:(g + 1) * D_STATE]
            xcol = xdt_ref[0, r0:r0 + rows, jb:jb + 1]
            dcol = dec_ref[0, r0:r0 + rows, jb:jb + 1]
            hn = dcol * h_ref[jb, r0:r0 + rows, :] + xcol * brow
            hout_ref[jb, r0:r0 + rows, :] = hn
            ycols = jnp.where(lane == jb, jnp.sum(hn * crow, axis=-1, keepdims=True), ycols)
        y_ref[0, r0:r0 + rows, :] = ycols


def _step_state(h0, xdt, dec, xc):
    m = h0.shape[0]
    bt = _STEP_BT
    assert m % bt == 0
    nb = m // bt
    cols = lambda a: a.reshape(nb, bt, D_INNER).transpose(0, 2, 1)
    h_new, y_cols = pl.pallas_call(
        _step_state_body,
        grid=(nb,),
        in_specs=[
            pl.BlockSpec((bt, D_INNER, D_STATE), lambda i: (i, 0, 0)),
            pl.BlockSpec((1, D_INNER, bt), lambda i: (i, 0, 0)),
            pl.BlockSpec((1, D_INNER, bt), lambda i: (i, 0, 0)),
            pl.BlockSpec((bt, 512), lambda i: (i, D_INNER // 512)),
            pl.BlockSpec((bt, 512), lambda i: (i, D_INNER // 512 + 1))],
        out_specs=[
            pl.BlockSpec((bt, D_INNER, D_STATE), lambda i: (i, 0, 0)),
            pl.BlockSpec((1, D_INNER, bt), lambda i: (i, 0, 0))],
        out_shape=[
            jax.ShapeDtypeStruct((m, D_INNER, D_STATE), F32),
            jax.ShapeDtypeStruct((nb, D_INNER, bt), F32)],
        compiler_params=_cparams(1),
    )(h0, cols(xdt), cols(dec), xc, xc)
    return h_new, y_cols.transpose(0, 2, 1).reshape(m, D_INNER)


def _step_post_body(y_ref, xc_ref, z_ref, dskip_ref, ng_ref, yn_ref):
    gw = D_INNER // SSM_GROUPS
    for g in range(SSM_GROUPS):
        gs = slice(g * gw, (g + 1) * gw)
        y = y_ref[:, gs] + dskip_ref[:, gs] * xc_ref[:, gs]
        y = y * _silu(z_ref[:, gs])
        ms = jnp.mean(y * y, axis=-1, keepdims=True)
        yn_ref[:, gs] = (y * lax.rsqrt(ms + NORM_EPS) * ng_ref[:, gs]).astype(BF16)


def _step_post(y, xc, proj, dskip_x, norm_g):
    m = y.shape[0]
    return pl.pallas_call(
        _step_post_body,
        grid=(1,),
        in_specs=[
            pl.BlockSpec((m, D_INNER), lambda i: (0, 0)),
            pl.BlockSpec((m, D_INNER), lambda i: (0, 0)),
            pl.BlockSpec((m, D_INNER), lambda i: (0, COL_Z // D_INNER)),
            pl.BlockSpec((1, D_INNER), lambda i: (0, 0)),
            pl.BlockSpec((1, D_INNER), lambda i: (0, 0))],
        out_specs=pl.BlockSpec((m, D_INNER), lambda i: (0, 0)),
        out_shape=jax.ShapeDtypeStruct((m, D_INNER), BF16),
        compiler_params=_cparams(1),
    )(y, xc, proj, dskip_x, norm_g)


def _row_to_col(row):
    n = row.shape[1]
    r = lax.broadcasted_iota(jnp.int32, (n, n), 0)
    c = lax.broadcasted_iota(jnp.int32, (n, n), 1)
    return jnp.sum(jnp.where(r == c, jnp.broadcast_to(row, (n, n)), 0.0), axis=1, keepdims=True)


def _col_to_row(col):
    n = col.shape[0]
    r = lax.broadcasted_iota(jnp.int32, (n, n), 0)
    c = lax.broadcasted_iota(jnp.int32, (n, n), 1)
    return jnp.sum(jnp.where(r == c, jnp.broadcast_to(col, (n, n)), 0.0), axis=0, keepdims=True)


def _step_window_body(q_ref, kn_ref, vn_ref, win_ref, o_ref, l_ref, wout_ref, *, group, bt):
    window, dil = DIL_GROUPS[group]
    dh = ATTN_HEAD_DIM
    wq = ATTN_OUT_WIDTH
    j = lax.broadcasted_iota(jnp.int32, (1, window), 1)
    read = lax.rem(j, dil) == 0
    dist = (window - j).astype(F32)
    last = lax.broadcasted_iota(jnp.int32, (wq, window), 1) == window - 1
    for jb in range(bt):
        q = q_ref[jb] * (ATTN_HEAD_DIM ** -0.5)
        k_new = kn_ref[jb]
        v_new = vn_ref[jb]
        k_t = win_ref[jb, 0]
        v_t = win_ref[jb, 1]
        qk = k_t * _row_to_col(q)
        ps, extras = [], []
        for h in range(HEADS_PER_DIL_GROUP):
            hs = slice(h * dh, (h + 1) * dh)
            s = jnp.sum(qk[hs, :], axis=0, keepdims=True) - _slope(group * HEADS_PER_DIL_GROUP + h) * dist
            s = jnp.where(read, s, -jnp.inf)
            s_new = jnp.sum(q[:, hs] * k_new[:, hs], axis=-1, keepdims=True)
            m = jnp.maximum(jnp.max(s, axis=-1, keepdims=True), s_new)
            p = jnp.exp(s - m)
            p_new = jnp.exp(s_new - m)
            l = jnp.sum(p, axis=-1, keepdims=True) + p_new
            ps.append(jnp.broadcast_to(p, (dh, window)))
            extras.append((p_new, l, m))
        pv = jnp.sum(v_t * jnp.concatenate(ps, axis=0), axis=1, keepdims=True)
        o_row = _col_to_row(pv)
        for h, (p_new, l, m) in enumerate(extras):
            hs = slice(h * dh, (h + 1) * dh)
            o_ref[jb, :, hs] = (o_row[:, hs] + p_new * v_new[:, hs]) / l
            l_ref[jb, :, hs] = jnp.broadcast_to(m + jnp.log(l), (1, dh))
        wout_ref[jb, 0] = jnp.where(last, _row_to_col(k_new), pltpu.roll(k_t, window - 1, 1))
        wout_ref[jb, 1] = jnp.where(last, _row_to_col(v_new), pltpu.roll(v_t, window - 1, 1))


def _step_window(q, k_new, v_new, win_t, group):
    m = q.shape[0]
    wq = ATTN_OUT_WIDTH
    window = win_t.shape[-1]
    bt = max(1, min(_STEP_BT, (4 * 1024 * 1024) // (2 * wq * window * 4)))
    assert m % bt == 0
    r3 = lambda a: a.reshape(m, 1, wq)
    vec = pl.BlockSpec((bt, 1, wq), lambda i: (i, 0, 0))
    win = pl.BlockSpec((bt, 2, wq, window), lambda i: (i, 0, 0, 0))
    o, lse, win_new = pl.pallas_call(
        functools.partial(_step_window_body, group=group, bt=bt),
        grid=(m // bt,),
        in_specs=[vec, vec, vec, win],
        out_specs=[vec, vec, win],
        out_shape=[jax.ShapeDtypeStruct((m, 1, wq), F32), jax.ShapeDtypeStruct((m, 1, wq), F32),
                   jax.ShapeDtypeStruct(win_t.shape, F32)],
        compiler_params=_cparams(1),
    )(r3(q), r3(k_new), r3(v_new), win_t)
    return o.reshape(m, wq), lse.reshape(m, wq), win_new


def _prep_weights(w_in, conv_b, dt_bias, a_log, d_skip, ssm_norm_g, w_out_ssm, w_out_attn, w_out,
                  ln1_g, ln1_b, w_router, b_router, ln2_g, ln2_b):
    cuts = np.cumsum((D_INNER, CONV_DIM, SSM_HEADS, ATTN_WIDTH, ATTN_WIDTH, ATTN_WIDTH, D_MODEL, D_MODEL))
    z, xbc, dt, q, k, v, ga, gb = jnp.split(w_in, [int(c) for c in cuts[:-1]], axis=1)
    xs, bm, cm = jnp.split(xbc, [D_INNER, D_INNER + 512], axis=1)
    dt_pad = jnp.zeros((D_MODEL, PROJ_W - COL_DT - SSM_HEADS), w_in.dtype)
    w_perm = jnp.concatenate([z, xs, ga, gb, bm, cm, q, k, v, dt, dt_pad], axis=1).astype(BF16)
    pad_heads = lambda a: jnp.pad(a.astype(F32), (0, LANES - SSM_HEADS)).reshape(1, LANES)
    row = lambda a: a.astype(F32).reshape(1, -1)
    return dict(
        w_perm=w_perm, conv_b=row(conv_b), dtb=pad_heads(dt_bias), alog=pad_heads(a_log),
        dskip=row(jnp.repeat(d_skip, SSM_HEAD_DIM)), norm_g=row(ssm_norm_g),
        w_ssm=w_out_ssm.astype(BF16), w_attn=w_out_attn.astype(BF16), w_out=w_out.astype(BF16),
        g1=row(ln1_g), b1=row(ln1_b),
        w_r=jnp.pad(w_router, ((0, 0), (0, LANES - N_EXPERTS))).astype(BF16),
        b_r=jnp.pad(b_router.astype(F32), (0, LANES - N_EXPERTS)).reshape(1, LANES),
        g2=row(ln2_g), b2=row(ln2_b))


def _layer_prompt(x, p, conv_w, moe_w):
    bsz, t, _ = x.shape
    x2d = x.reshape(bsz * t, D_MODEL)
    proj = _in_proj(x2d, p['w_perm'])
    yn, h_new, conv_new = _ssd_prompt(proj, bsz, t, conv_w, p['conv_b'], p['dtb'], p['alog'],
                                      p['dskip'], p['norm_g'])
    attn, kvs = [], []
    for g, (window, _) in enumerate(DIL_GROUPS):
        o, lse, kvt = _attn_prompt(proj, bsz, t, g)
        attn.extend((o, lse))
        kvs.append(_window_from_lanes(kvt))
    x1t, idx_dense, gates_dense, rank_dense, counts = _mix(
        yn, attn, proj, x2d, p['w_ssm'], p['w_attn'], p['w_out'], p['g1'], p['b1'], p['w_r'], p['b_r'])
    y = _moe_and_norm(x1t, idx_dense, gates_dense, rank_dense, counts, *moe_w, p['g2'], p['b2'])
    return y.reshape(bsz, t, D_MODEL), h_new, conv_new, kvs


def _layer_step(x, conv_buf, h0, kv_bufs, p, conv_w, moe_w):
    m = x.shape[0]
    x2d = x.reshape(m, D_MODEL)
    proj = _in_proj(x2d, p['w_perm'])
    xc, conv_new_t, xdt, dec = _step_pre(proj, conv_buf.transpose(1, 0, 2), conv_w, p['conv_b'],
                                         p['dtb'], p['alog'])
    h_new, y = _step_state(h0.reshape(m, D_INNER, D_STATE), xdt, dec, xc)
    yn = _step_post(y, xc, proj, p['dskip'], p['norm_g'])
    wq = ATTN_OUT_WIDTH
    attn, kv_new = [], []
    for g in range(len(DIL_GROUPS)):
        cols = lambda c0: proj[:, c0 + g * wq:c0 + (g + 1) * wq]
        o, lse, win_new = _step_window(cols(COL_Q), cols(COL_K), cols(COL_V),
                                       _window_to_lanes(kv_bufs[g]), g)
        attn.extend((o, lse))
        kv_new.append(_window_from_lanes(win_new))
    x1t, idx_dense, gates_dense, rank_dense, counts = _mix(
        yn, attn, proj, x2d, p['w_ssm'], p['w_attn'], p['w_out'], p['g1'], p['b1'], p['w_r'], p['b_r'])
    y_out = _moe_and_norm(x1t, idx_dense, gates_dense, rank_dense, counts, *moe_w, p['g2'], p['b2'])
    h_new = h_new.reshape(m, SSM_HEADS, SSM_HEAD_DIM, D_STATE)
    return y_out.reshape(m, 1, D_MODEL), h_new, conv_new_t.transpose(1, 0, 2), kv_new


def kernel(x_prompt, x_sample, state_ssm, state_conv, cache_kv_w128, cache_kv_w512, cache_kv_w2048, w_in, conv_w, conv_b, dt_bias, a_log, d_skip, ssm_norm_g, w_out_ssm, w_out_attn, w_out, ln1_g, ln1_b, w_router, b_router, w_gate_up, b_gate_up, w_down, b_down, ln2_g, ln2_b):
    assert w_in.shape[0] == DEPTH == 1 and x_sample.shape[1] == 1
    l = 0
    p = _prep_weights(w_in[l], conv_b[l], dt_bias[l], a_log[l], d_skip[l], ssm_norm_g[l], w_out_ssm[l],
                      w_out_attn[l], w_out[l], ln1_g[l], ln1_b[l], w_router[l], b_router[l], ln2_g[l],
                      ln2_b[l])
    moe_w = (w_gate_up[l], b_gate_up[l], w_down[l], b_down[l])
    y_p, hp, cp, kvp = _layer_prompt(x_prompt, p, conv_w[l], moe_w)
    y_s, hs, cs, kvs = _layer_step(x_sample, state_conv[l], state_ssm[l],
                                   (cache_kv_w128[l], cache_kv_w512[l], cache_kv_w2048[l]), p,
                                   conv_w[l], moe_w)
    stack = lambda a: a[None]
    return (y_p, y_s, stack(hp), stack(cp), stack(kvp[0]), stack(kvp[1]), stack(kvp[2]),
            stack(hs), stack(cs), stack(kvs[0]), stack(kvs[1]), stack(kvs[2]))
```
